```python
import jax, jax.numpy as jnp
from jax import lax
import numpy as np

D_MODEL = 1024
BATCH = 8
SEQ = 2048
DEPTH = 2
DEC_BATCH = 32
DEC_SEQ = 1
PAST_LEN = 16384
PAGE_SIZE = 128

DN_HEADS = 4
DN_DK = 128
DN_DV = 128
CONV_W = 4
DN_CHUNK = 64
ATT_HEADS = 8
KV_HEADS = 2
HEAD_DIM = 64
IDX_HEADS = 4
IDX_DIM = 64
TOPK_MAX = 256
ROPE_THETA = 500000.0
ROT_FRACTION = 4
Q_BLOCK = 128
MEM_LEN = 256
MEM_HEADS = 4
MEM_HEAD_DIM = 128
N_KEYS = 128
N_EXPERTS = N_KEYS * N_KEYS
PEER_HEADS = 8
PEER_QDIM = 256
PEER_TOPK = 16
PEER_BLOCK = 128
NORM_EPS = 1e-6

DN_QK_W = DN_HEADS * DN_DK
DN_V_W = DN_HEADS * DN_DV
CONV_DIM = 2 * DN_QK_W + DN_V_W
ATT_Q_W = ATT_HEADS * HEAD_DIM
ATT_KV_W = KV_HEADS * HEAD_DIM
IDX_Q_W = IDX_HEADS * IDX_DIM
MEM_W = MEM_HEADS * MEM_HEAD_DIM
IN_SPLITS = (CONV_DIM, DN_V_W, DN_HEADS, DN_HEADS, ATT_Q_W, ATT_KV_W, ATT_KV_W, IDX_Q_W, IDX_DIM, IDX_HEADS, D_MODEL, D_MODEL)
N_IN = CONV_DIM + DN_V_W + 2 * DN_HEADS + ATT_Q_W + 2 * ATT_KV_W + IDX_Q_W + IDX_DIM + IDX_HEADS + 2 * D_MODEL
F32 = jnp.float32

kernel_name = 'hybrid_deltanet_dsa_peer_decoder_step'


def rms_norm(x, g):
    xf = x.astype(F32)
    y = xf * lax.rsqrt(jnp.mean(xf * xf, axis=-1, keepdims=True) + NORM_EPS)
    return (y * g.astype(F32)).astype(x.dtype)


def l2_normalize(x):
    return x * lax.rsqrt(jnp.sum(x * x, axis=-1, keepdims=True) + NORM_EPS)


def rope_partial(x, pos):
    rot = x.shape[-1] // ROT_FRACTION
    half = rot // 2
    inv_freq = ROPE_THETA ** (-jnp.arange(half, dtype=F32) / half)
    ang = pos.astype(F32)[:, None] * inv_freq[None, :]
    cos = jnp.cos(ang)[:, None, :]
    sin = jnp.sin(ang)[:, None, :]
    xr = x[..., :rot].astype(F32)
    x1, x2 = xr[..., :half], xr[..., half:]
    rotated = jnp.concatenate([x1 * cos - x2 * sin, x2 * cos + x1 * sin], axis=-1)
    return jnp.concatenate([rotated.astype(x.dtype), x[..., rot:]], axis=-1)


def split_columns(z):
    offsets = np.cumsum(IN_SPLITS)[:-1].tolist()
    return jnp.split(z, offsets, axis=-1)


def causal_dwconv_silu(xp, w):
    y = lax.conv_general_dilated(xp, w[:, None, :].astype(xp.dtype), window_strides=(1,), padding='VALID',
                                 dimension_numbers=('NWC', 'WIO', 'NWC'), feature_group_count=xp.shape[-1])
    return jax.nn.silu(y)


def dn_features(c, a, b, a_log, dt_bias):
    B, T, _ = c.shape
    cf = c.astype(F32)
    q = l2_normalize(cf[..., :DN_QK_W].reshape(B, T, DN_HEADS, DN_DK)) * (DN_DK ** -0.5)
    k = l2_normalize(cf[..., DN_QK_W:2 * DN_QK_W].reshape(B, T, DN_HEADS, DN_DK))
    v = cf[..., 2 * DN_QK_W:].reshape(B, T, DN_HEADS, DN_DV)
    beta = jax.nn.sigmoid(b.astype(F32))
    g = -jnp.exp(a_log.astype(F32)) * jax.nn.softplus(a.astype(F32) + dt_bias.astype(F32))
    return q, k, v, g, beta


def dn_output(o, z, gain, dtype):
    B, T = o.shape[:2]
    on = o * lax.rsqrt(jnp.mean(o * o, axis=-1, keepdims=True) + NORM_EPS) * gain.astype(F32)
    y = on * jax.nn.silu(z.astype(F32)).reshape(B, T, DN_HEADS, DN_DV)
    return y.reshape(B, T, DN_V_W).astype(dtype)


def gated_delta_chunked(q, k, v, g, beta):
    B, T, H, DK = q.shape
    DV = v.shape[-1]
    C = DN_CHUNK
    NC = T // C

    def chunks(x):
        return jnp.moveaxis(x.reshape((B, NC, C, H) + x.shape[3:]), 3, 1)

    qc, kc, vc, bc = chunks(q), chunks(k), chunks(v), chunks(beta)
    gc = jnp.cumsum(chunks(g), axis=-1)
    tri = jnp.tril(jnp.ones((C, C), bool))
    strict = jnp.tril(jnp.ones((C, C), bool), -1)
    diff = gc[..., :, None] - gc[..., None, :]
    decay = jnp.where(tri, jnp.exp(jnp.where(tri, diff, 0.0)), 0.0)
    kk = jnp.einsum('bhnid,bhnjd->bhnij', kc, kc)
    a_mat = jnp.where(strict, bc[..., :, None] * kk * decay, 0.0) + jnp.eye(C, dtype=F32)
    rhs = jnp.concatenate([vc * bc[..., None], kc * (bc * jnp.exp(gc))[..., None]], axis=-1)
    sol = lax.linalg.triangular_solve(a_mat, rhs, left_side=True, lower=True)
    val, kcd = sol[..., :DV], sol[..., DV:]
    a_qk = jnp.einsum('bhnid,bhnjd->bhnij', qc, kc) * decay
    q_dec = qc * jnp.exp(gc)[..., None]
    g_last = gc[..., -1]
    k_dec = kc * jnp.exp(g_last[..., None] - gc)[..., None]

    def step(S, xs):
        qd, kd, vl, kcd_c, aqk, gl = xs
        v_new = vl - jnp.einsum('bhck,bhkv->bhcv', kcd_c, S)
        o = jnp.einsum('bhck,bhkv->bhcv', qd, S) + jnp.einsum('bhcj,bhjv->bhcv', aqk, v_new)
        S = S * jnp.exp(gl)[..., None, None] + jnp.einsum('bhck,bhcv->bhkv', kd, v_new)
        return S, o

    xs = tuple(jnp.moveaxis(t, 2, 0) for t in (q_dec, k_dec, val, kcd, a_qk, g_last))
    S_fin, o = lax.scan(step, jnp.zeros((B, H, DK, DV), F32), xs)
    o = jnp.transpose(o, (1, 0, 3, 2, 4)).reshape(B, T, H, DV)
    return o, S_fin


def gated_delta_recurrent(q, k, v, g, beta, S0):
    def step(S, xs):
        qt, kt, vt, gt, bt = xs
        S = S * jnp.exp(gt)[..., None, None]
        pred = jnp.einsum('bhk,bhkv->bhv', kt, S)
        S = S + jnp.einsum('bhk,bhv->bhkv', kt, bt[..., None] * (vt - pred))
        return S, jnp.einsum('bhk,bhkv->bhv', qt, S)

    xs = tuple(jnp.moveaxis(t, 1, 0) for t in (q, k, v, g, beta))
    S_fin, o = lax.scan(step, S0, xs)
    return jnp.moveaxis(o, 0, 1), S_fin


def dsa_features(q, k, v, qi, ki, pos):
    B, T, _ = q.shape
    q = rope_partial(q.reshape(B, T, ATT_HEADS, HEAD_DIM), pos)
    k = rope_partial(k.reshape(B, T, KV_HEADS, HEAD_DIM), pos)
    v = v.reshape(B, T, KV_HEADS, HEAD_DIM)
    qi = rope_partial(qi.reshape(B, T, IDX_HEADS, IDX_DIM), pos)
    ki = rope_partial(ki.reshape(B, T, 1, IDX_DIM), pos)[:, :, 0]
    return q, k, v, qi, ki


def index_scores(qi, ki, wi):
    s = jax.nn.relu(jnp.einsum('bqhd,bsd->bqhs', qi, ki).astype(F32) * (IDX_DIM ** -0.5))
    return jnp.einsum('bqh,bqhs->bqs', wi.astype(F32) * (IDX_HEADS ** -0.5), s)


def sparse_attend(q, ks, vs, valid):
    B, Q, H, HD = q.shape
    qg = q.reshape(B, Q, KV_HEADS, H // KV_HEADS, HD)
    s = jnp.einsum('bqngd,bqknd->bqngk', qg, ks).astype(F32) * (HD ** -0.5)
    s = jnp.where(valid[:, :, None, None, :], s, -jnp.inf)
    p = jax.nn.softmax(s, axis=-1).astype(vs.dtype)
    o = jnp.einsum('bqngk,bqknd->bqngd', p, vs)
    return o.reshape(B, Q, H * HD)


def dsa_prompt(q, k, v, qi, ki, wi, topk):
    B, T = q.shape[:2]
    nb = T // Q_BLOCK
    bidx = jnp.arange(B)[:, None, None]
    kpos = jnp.arange(T)

    def blocks(x):
        return jnp.swapaxes(x.reshape((B, nb, Q_BLOCK) + x.shape[2:]), 0, 1)

    def one_block(args):
        qb, qib, wib, t0 = args
        tpos = t0 + jnp.arange(Q_BLOCK)
        visible = kpos[None, :] <= tpos[:, None]
        score = jnp.where(visible[None], index_scores(qib, ki, wib), -jnp.inf)
        _, sel = lax.top_k(score, topk)
        valid = sel <= tpos[None, :, None]
        return sparse_attend(qb, k[bidx, sel], v[bidx, sel], valid)

    out = lax.map(one_block, (blocks(q), blocks(qi), blocks(wi), jnp.arange(nb) * Q_BLOCK))
    return jnp.swapaxes(out, 0, 1).reshape(B, T, ATT_Q_W)


def dsa_sample(q, k_new, v_new, qi, ki_new, wi, cache_k, cache_v, cache_kidx, page_table, layer, topk):
    B, S = q.shape[:2]
    past = page_table.shape[1] * PAGE_SIZE
    slots = jnp.arange(PAGE_SIZE)
    ki_past = cache_kidx[page_table[:, :, None], slots[None, None, :], layer].reshape(B, past, IDX_DIM)
    ki_all = jnp.concatenate([ki_past.astype(ki_new.dtype), ki_new], axis=1)
    qpos = past + jnp.arange(S)
    kpos = jnp.arange(past + S)
    visible = kpos[None, :] <= qpos[:, None]
    score = jnp.where(visible[None], index_scores(qi, ki_all, wi), -jnp.inf)
    _, sel = lax.top_k(score, topk)
    in_past = sel < past
    ps = jnp.minimum(sel, past - 1)
    phys = jnp.take_along_axis(page_table, (ps // PAGE_SIZE).reshape(B, S * topk), axis=1).reshape(B, S, topk)
    slot = ps % PAGE_SIZE
    bidx = jnp.arange(B)[:, None, None]
    ns = jnp.clip(sel - past, 0, S - 1)
    sel_k = jnp.where(in_past[..., None, None], cache_k[phys, slot, layer].astype(k_new.dtype), k_new[bidx, ns])
    sel_v = jnp.where(in_past[..., None, None], cache_v[phys, slot, layer].astype(v_new.dtype), v_new[bidx, ns])
    valid = sel <= qpos[None, :, None]
    return sparse_attend(q, sel_k, sel_v, valid)


def branch_merge(o_dn, o_att, g_dn, g_att, w_dn_up, w_att_up, w_o):
    y = jax.nn.sigmoid(g_dn) * (o_dn @ w_dn_up) + jax.nn.sigmoid(g_att) * (o_att @ w_att_up)
    return y @ w_o


def mem_kv(mem, g, w_mk, w_mv):
    B, M, _ = mem.shape
    m = rms_norm(mem, g)
    return ((m @ w_mk).reshape(B, M, MEM_HEADS, MEM_HEAD_DIM), (m @ w_mv).reshape(B, M, MEM_HEADS, MEM_HEAD_DIM))


def mem_attend(x, mk, mv, g, w_mq, w_mo):
    B, T, _ = x.shape
    q = (rms_norm(x, g) @ w_mq).reshape(B, T, MEM_HEADS, MEM_HEAD_DIM)
    s = jnp.einsum('bthd,bmhd->bhtm', q, mk.astype(q.dtype)).astype(F32) * (MEM_HEAD_DIM ** -0.5)
    p = jax.nn.softmax(s, axis=-1).astype(q.dtype)
    o = jnp.einsum('bhtm,bmhd->bthd', p, mv.astype(q.dtype)).reshape(B, T, MEM_W)
    return o @ w_mo


def peer_block(h, wq, sub_keys, u, v):
    n = h.shape[0]
    q = (h @ wq).reshape(n, PEER_HEADS, 2, PEER_QDIM // 2)
    s = jnp.einsum('nhcd,hckd->nhck', q, sub_keys).astype(F32)
    s_top, i_top = lax.top_k(s, PEER_TOPK)
    cand = (s_top[:, :, 0, :, None] + s_top[:, :, 1, None, :]).reshape(n, PEER_HEADS, PEER_TOPK * PEER_TOPK)
    cidx = (i_top[:, :, 0, :, None] * N_KEYS + i_top[:, :, 1, None, :]).reshape(n, PEER_HEADS, PEER_TOPK * PEER_TOPK)
    best, pos = lax.top_k(cand, PEER_TOPK)
    eidx = jnp.take_along_axis(cidx, pos, axis=-1).reshape(n, PEER_HEADS * PEER_TOPK)
    gate = jax.nn.softmax(best, axis=-1).reshape(n, PEER_HEADS * PEER_TOPK)
    act = jax.nn.gelu(jnp.einsum('nd,ned->ne', h, u[eidx]).astype(F32), approximate=False)
    return jnp.einsum('ne,ned->nd', (gate * act).astype(h.dtype), v[eidx])


def peer_ffn(h, wq, sub_keys, u, v):
    B, T, D = h.shape
    n = B * T
    flat = h.reshape(n, D)
    if n % PEER_BLOCK == 0 and n > PEER_BLOCK:
        out = lax.map(lambda hb: peer_block(hb, wq, sub_keys, u, v), flat.reshape(n // PEER_BLOCK, PEER_BLOCK, D))
        out = out.reshape(n, D)
    else:
        out = peer_block(flat, wq, sub_keys, u, v)
    return out.reshape(B, T, D)


def setup_inputs(seed: int = 0) -> dict:
    key = jax.random.key(seed)
    ks = iter(jax.random.split(key, 48))

    def nrm(shape, scale=1.0):
        return jax.random.normal(next(ks), shape, F32) * scale

    def gain(shape):
        return 1.0 + nrm(shape, 0.01)

    n_pages = PAST_LEN // PAGE_SIZE
    n_used = DEC_BATCH * n_pages
    n_pool = n_used + max(1, n_used // 4)
    perm = jax.random.permutation(next(ks), n_pool)
    page_table = perm[:n_used].reshape(DEC_BATCH, n_pages).astype(jnp.int32)
    dt = jnp.exp(jax.random.uniform(next(ks), (DEPTH, DN_HEADS), F32, np.log(1e-3), np.log(1e-1)))
    return {
        'x_prompt': nrm((BATCH, SEQ, D_MODEL)),
        'x_sample': nrm((DEC_BATCH, DEC_SEQ, D_MODEL)),
        'cache_k': nrm((n_pool, PAGE_SIZE, DEPTH, KV_HEADS, HEAD_DIM)),
        'cache_v': nrm((n_pool, PAGE_SIZE, DEPTH, KV_HEADS, HEAD_DIM)),
        'cache_kidx': nrm((n_pool, PAGE_SIZE, DEPTH, IDX_DIM)),
        'state_conv': nrm((DEPTH, DEC_BATCH, CONV_W - 1, CONV_DIM)),
        'state_delta': nrm((DEPTH, DEC_BATCH, DN_HEADS, DN_DK, DN_DV), 0.1),
        'cache_mem_k': nrm((DEPTH, DEC_BATCH, MEM_LEN, MEM_HEADS, MEM_HEAD_DIM)),
        'cache_mem_v': nrm((DEPTH, DEC_BATCH, MEM_LEN, MEM_HEADS, MEM_HEAD_DIM)),
        'page_table': page_table,
        'mem_prompt': nrm((BATCH, MEM_LEN, D_MODEL)),
        'norm_mix': gain((DEPTH, D_MODEL)),
        'w_in': nrm((DEPTH, D_MODEL, N_IN), D_MODEL ** -0.5),
        'dn_conv_w': nrm((DEPTH, CONV_W, CONV_DIM), CONV_W ** -0.5),
        'dn_a_log': jnp.log(jax.random.uniform(next(ks), (DEPTH, DN_HEADS), F32, 1.0, 16.0)),
        'dn_dt_bias': dt + jnp.log(-jnp.expm1(-dt)),
        'dn_norm': gain((DEPTH, DN_DV)),
        'w_dn_up': nrm((DEPTH, DN_V_W, D_MODEL), DN_V_W ** -0.5),
        'w_att_up': nrm((DEPTH, ATT_Q_W, D_MODEL), ATT_Q_W ** -0.5),
        'w_o': nrm((DEPTH, D_MODEL, D_MODEL), D_MODEL ** -0.5),
        'norm_mem_q': gain((DEPTH, D_MODEL)),
        'norm_mem_kv': gain((DEPTH, D_MODEL)),
        'w_mq': nrm((DEPTH, D_MODEL, MEM_W), D_MODEL ** -0.5),
        'w_mk': nrm((DEPTH, D_MODEL, MEM_W), D_MODEL ** -0.5),
        'w_mv': nrm((DEPTH, D_MODEL, MEM_W), D_MODEL ** -0.5),
        'w_mo': nrm((DEPTH, MEM_W, D_MODEL), MEM_W ** -0.5),
        'norm_ffn': gain((DEPTH, D_MODEL)),
        'peer_wq': nrm((DEPTH, D_MODEL, PEER_HEADS * PEER_QDIM), D_MODEL ** -0.5),
        'peer_keys': nrm((DEPTH, PEER_HEADS, 2, N_KEYS, PEER_QDIM // 2), (PEER_QDIM // 2) ** -0.5),
        'peer_u': nrm((DEPTH, N_EXPERTS, D_MODEL), D_MODEL ** -0.5),
        'peer_v': nrm((DEPTH, N_EXPERTS, D_MODEL), (PEER_HEADS * PEER_TOPK) ** -0.5),
        'norm_final': gain((D_MODEL,)),
    }


def reference(x_prompt, x_sample, cache_k, cache_v, cache_kidx, state_conv, state_delta, cache_mem_k, cache_mem_v,
              page_table, mem_prompt, norm_mix, w_in, dn_conv_w, dn_a_log, dn_dt_bias, dn_norm, w_dn_up, w_att_up,
              w_o, norm_mem_q, norm_mem_kv, w_mq, w_mk, w_mv, w_mo, norm_ffn, peer_wq, peer_keys, peer_u, peer_v,
              norm_final):
    B, T, _ = x_prompt.shape
    S = x_sample.shape[1]
    past = page_table.shape[1] * PAGE_SIZE
    pos_p = jnp.arange(T)
    pos_s = past + jnp.arange(S)
    topk_p = min(TOPK_MAX, T // 4)
    topk_s = min(TOPK_MAX, (past + S) // 4)

    hp, hs = x_prompt, x_sample
    p_k, p_v, p_kidx, p_conv, p_delta, p_mk, p_mv = [], [], [], [], [], [], []
    s_k, s_v, s_kidx, s_conv, s_delta = [], [], [], [], []
    for l in range(DEPTH):
        (conv_in, z, a, b, q, k, v, qi, ki, wi, g_dn, g_att) = split_columns(rms_norm(hp, norm_mix[l]) @ w_in[l])
        cbuf = jnp.concatenate([jnp.zeros((B, CONV_W - 1, CONV_DIM), conv_in.dtype), conv_in], axis=1)
        dq, dk, dv, dg, dbeta = dn_features(causal_dwconv_silu(cbuf, dn_conv_w[l]), a, b, dn_a_log[l], dn_dt_bias[l])
        o, st = gated_delta_chunked(dq, dk, dv, dg, dbeta)
        o_dn = dn_output(o, z, dn_norm[l], hp.dtype)
        aq, ak, av, aqi, aki = dsa_features(q, k, v, qi, ki, pos_p)
        o_att = dsa_prompt(aq, ak, av, aqi, aki, wi, topk_p)
        hp = hp + branch_merge(o_dn, o_att, g_dn, g_att, w_dn_up[l], w_att_up[l], w_o[l])
        mk, mv = mem_kv(mem_prompt, norm_mem_kv[l], w_mk[l], w_mv[l])
        hp = hp + mem_attend(hp, mk, mv, norm_mem_q[l], w_mq[l], w_mo[l])
        hp = hp + peer_ffn(rms_norm(hp, norm_ffn[l]), peer_wq[l], peer_keys[l], peer_u[l], peer_v[l])
        p_k.append(ak)
        p_v.append(av)
        p_kidx.append(aki)
        p_conv.append(cbuf[:, -(CONV_W - 1):])
        p_delta.append(st)
        p_mk.append(mk)
        p_mv.append(mv)

        (conv_in, z, a, b, q, k, v, qi, ki, wi, g_dn, g_att) = split_columns(rms_norm(hs, norm_mix[l]) @ w_in[l])
        cbuf = jnp.concatenate([state_conv[l].astype(conv_in.dtype), conv_in], axis=1)
        dq, dk, dv, dg, dbeta = dn_features(causal_dwconv_silu(cbuf, dn_conv_w[l]), a, b, dn_a_log[l], dn_dt_bias[l])
        o, st = gated_delta_recurrent(dq, dk, dv, dg, dbeta, state_delta[l].astype(F32))
        o_dn = dn_output(o, z, dn_norm[l], hs.dtype)
        aq, ak, av, aqi, aki = dsa_features(q, k, v, qi, ki, pos_s)
        o_att = dsa_sample(aq, ak, av, aqi, aki, wi, cache_k, cache_v, cache_kidx, page_table, l, topk_s)
        hs = hs + branch_merge(o_dn, o_att, g_dn, g_att, w_dn_up[l], w_att_up[l], w_o[l])
        hs = hs + mem_attend(hs, cache_mem_k[l], cache_mem_v[l], norm_mem_q[l], w_mq[l], w_mo[l])
        hs = hs + peer_ffn(rms_norm(hs, norm_ffn[l]), peer_wq[l], peer_keys[l], peer_u[l], peer_v[l])
        s_k.append(ak)
        s_v.append(av)
        s_kidx.append(aki)
        s_conv.append(cbuf[:, -(CONV_W - 1):])
        s_delta.append(st)

    y_prompt = rms_norm(hp, norm_final)
    y_sample = rms_norm(hs, norm_final)
    return (y_prompt, y_sample,
            jnp.stack(p_k, axis=2), jnp.stack(p_v, axis=2), jnp.stack(p_kidx, axis=2),
            jnp.stack(p_conv, axis=0), jnp.stack(p_delta, axis=0),
            jnp.stack(p_mk, axis=0), jnp.stack(p_mv, axis=0),
            jnp.stack(s_k, axis=2), jnp.stack(s_v, axis=2), jnp.stack(s_kidx, axis=2),
            jnp.stack(s_conv, axis=0), jnp.stack(s_delta, axis=0))
```

```python
import functools

import jax
import jax.numpy as jnp
import numpy as np
from jax import lax
from jax.experimental import pallas as pl
from jax.experimental.pallas import tpu as pltpu

D_MODEL = 1024
DEPTH = 2
PAGE_SIZE = 128
DN_HEADS = 4
DN_DK = 128
DN_DV = 128
CONV_W = 4
DN_CHUNK = 64
ATT_HEADS = 8
KV_HEADS = 2
HEAD_DIM = 64
IDX_HEADS = 4
IDX_DIM = 64
TOPK_MAX = 256
ROPE_THETA = 500000.0
ROT_FRACTION = 4
Q_BLOCK = 128
MEM_LEN = 256
MEM_HEADS = 4
MEM_HEAD_DIM = 128
N_KEYS = 128
N_EXPERTS = N_KEYS * N_KEYS
PEER_HEADS = 8
PEER_QDIM = 256
PEER_TOPK = 16
PEER_BLOCK = 128
NORM_EPS = 1e-6

DN_QK_W = DN_HEADS * DN_DK
DN_V_W = DN_HEADS * DN_DV
CONV_DIM = 2 * DN_QK_W + DN_V_W
ATT_Q_W = ATT_HEADS * HEAD_DIM
ATT_KV_W = KV_HEADS * HEAD_DIM
IDX_Q_W = IDX_HEADS * IDX_DIM
MEM_W = MEM_HEADS * MEM_HEAD_DIM
IN_SPLITS = (CONV_DIM, DN_V_W, DN_HEADS, DN_HEADS, ATT_Q_W, ATT_KV_W, ATT_KV_W, IDX_Q_W, IDX_DIM, IDX_HEADS,
             D_MODEL, D_MODEL)
F32 = jnp.float32


def _rms_kernel(x_ref, g_ref, o_ref):
    x = x_ref[...]
    ms = jnp.mean(x * x, axis=-1, keepdims=True)
    o_ref[...] = x * lax.rsqrt(ms + NORM_EPS) * g_ref[...]


def pallas_rms_norm(x2d, g, tm):
    n, d = x2d.shape
    return pl.pallas_call(
        _rms_kernel,
        grid=(n // tm,),
        in_specs=[pl.BlockSpec((tm, d), lambda i: (i, 0)), pl.BlockSpec((1, d), lambda i: (0, 0))],
        out_specs=pl.BlockSpec((tm, d), lambda i: (i, 0)),
        out_shape=jax.ShapeDtypeStruct((n, d), F32),
        name="rms_norm",
    )(x2d, g.reshape(1, d))


def rms_norm(x, g):
    xf = x.astype(F32)
    y = xf * lax.rsqrt(jnp.mean(xf * xf, axis=-1, keepdims=True) + NORM_EPS)
    return (y * g.astype(F32)).astype(x.dtype)


def l2_normalize(x):
    return x * lax.rsqrt(jnp.sum(x * x, axis=-1, keepdims=True) + NORM_EPS)


def rope_partial(x, pos):
    rot = x.shape[-1] // ROT_FRACTION
    half = rot // 2
    inv_freq = ROPE_THETA ** (-jnp.arange(half, dtype=F32) / half)
    ang = pos.astype(F32)[:, None] * inv_freq[None, :]
    cos = jnp.cos(ang)[:, None, :]
    sin = jnp.sin(ang)[:, None, :]
    xr = x[..., :rot].astype(F32)
    x1, x2 = xr[..., :half], xr[..., half:]
    rotated = jnp.concatenate([x1 * cos - x2 * sin, x2 * cos + x1 * sin], axis=-1)
    return jnp.concatenate([rotated.astype(x.dtype), x[..., rot:]], axis=-1)


def split_columns(z):
    offsets = np.cumsum(IN_SPLITS)[:-1].tolist()
    return jnp.split(z, offsets, axis=-1)


def causal_dwconv_silu(xp, w):
    y = lax.conv_general_dilated(xp, w[:, None, :].astype(xp.dtype), window_strides=(1,), padding='VALID',
                                 dimension_numbers=('NWC', 'WIO', 'NWC'), feature_group_count=xp.shape[-1])
    return jax.nn.silu(y)


def dn_features(c, a, b, a_log, dt_bias):
    B, T, _ = c.shape
    cf = c.astype(F32)
    q = l2_normalize(cf[..., :DN_QK_W].reshape(B, T, DN_HEADS, DN_DK)) * (DN_DK ** -0.5)
    k = l2_normalize(cf[..., DN_QK_W:2 * DN_QK_W].reshape(B, T, DN_HEADS, DN_DK))
    v = cf[..., 2 * DN_QK_W:].reshape(B, T, DN_HEADS, DN_DV)
    beta = jax.nn.sigmoid(b.astype(F32))
    g = -jnp.exp(a_log.astype(F32)) * jax.nn.softplus(a.astype(F32) + dt_bias.astype(F32))
    return q, k, v, g, beta


def dn_output(o, z, gain, dtype):
    B, T = o.shape[:2]
    on = o * lax.rsqrt(jnp.mean(o * o, axis=-1, keepdims=True) + NORM_EPS) * gain.astype(F32)
    y = on * jax.nn.silu(z.astype(F32)).reshape(B, T, DN_HEADS, DN_DV)
    return y.reshape(B, T, DN_V_W).astype(dtype)


def gated_delta_chunked(q, k, v, g, beta):
    B, T, H, DK = q.shape
    DV = v.shape[-1]
    C = DN_CHUNK
    NC = T // C

    def chunks(x):
        return jnp.moveaxis(x.reshape((B, NC, C, H) + x.shape[3:]), 3, 1)

    qc, kc, vc, bc = chunks(q), chunks(k), chunks(v), chunks(beta)
    gc = jnp.cumsum(chunks(g), axis=-1)
    tri = jnp.tril(jnp.ones((C, C), bool))
    strict = jnp.tril(jnp.ones((C, C), bool), -1)
    diff = gc[..., :, None] - gc[..., None, :]
    decay = jnp.where(tri, jnp.exp(jnp.where(tri, diff, 0.0)), 0.0)
    kk = jnp.einsum('bhnid,bhnjd->bhnij', kc, kc)
    a_mat = jnp.where(strict, bc[..., :, None] * kk * decay, 0.0) + jnp.eye(C, dtype=F32)
    rhs = jnp.concatenate([vc * bc[..., None], kc * (bc * jnp.exp(gc))[..., None]], axis=-1)
    sol = lax.linalg.triangular_solve(a_mat, rhs, left_side=True, lower=True)
    val, kcd = sol[..., :DV], sol[..., DV:]
    a_qk = jnp.einsum('bhnid,bhnjd->bhnij', qc, kc) * decay
    q_dec = qc * jnp.exp(gc)[..., None]
    g_last = gc[..., -1]
    k_dec = kc * jnp.exp(g_last[..., None] - gc)[..., None]

    def step(S, xs):
        qd, kd, vl, kcd_c, aqk, gl = xs
        v_new = vl - jnp.einsum('bhck,bhkv->bhcv', kcd_c, S)
        o = jnp.einsum('bhck,bhkv->bhcv', qd, S) + jnp.einsum('bhcj,bhjv->bhcv', aqk, v_new)
        S = S * jnp.exp(gl)[..., None, None] + jnp.einsum('bhck,bhcv->bhkv', kd, v_new)
        return S, o

    xs = tuple(jnp.moveaxis(t, 2, 0) for t in (q_dec, k_dec, val, kcd, a_qk, g_last))
    S_fin, o = lax.scan(step, jnp.zeros((B, H, DK, DV), F32), xs)
    o = jnp.transpose(o, (1, 0, 3, 2, 4)).reshape(B, T, H, DV)
    return o, S_fin


def gated_delta_recurrent(q, k, v, g, beta, S0):
    def step(S, xs):
        qt, kt, vt, gt, bt = xs
        S = S * jnp.exp(gt)[..., None, None]
        pred = jnp.einsum('bhk,bhkv->bhv', kt, S)
        S = S + jnp.einsum('bhk,bhv->bhkv', kt, bt[..., None] * (vt - pred))
        return S, jnp.einsum('bhk,bhkv->bhv', qt, S)

    xs = tuple(jnp.moveaxis(t, 1, 0) for t in (q, k, v, g, beta))
    S_fin, o = lax.scan(step, S0, xs)
    return jnp.moveaxis(o, 0, 1), S_fin


def dsa_features(q, k, v, qi, ki, pos):
    B, T, _ = q.shape
    q = rope_partial(q.reshape(B, T, ATT_HEADS, HEAD_DIM), pos)
    k = rope_partial(k.reshape(B, T, KV_HEADS, HEAD_DIM), pos)
    v = v.reshape(B, T, KV_HEADS, HEAD_DIM)
    qi = rope_partial(qi.reshape(B, T, IDX_HEADS, IDX_DIM), pos)
    ki = rope_partial(ki.reshape(B, T, 1, IDX_DIM), pos)[:, :, 0]
    return q, k, v, qi, ki


def index_scores(qi, ki, wi):
    s = jax.nn.relu(jnp.einsum('bqhd,bsd->bqhs', qi, ki).astype(F32) * (IDX_DIM ** -0.5))
    return jnp.einsum('bqh,bqhs->bqs', wi.astype(F32) * (IDX_HEADS ** -0.5), s)


def sparse_attend(q, ks, vs, valid):
    B, Q, H, HD = q.shape
    qg = q.reshape(B, Q, KV_HEADS, H // KV_HEADS, HD)
    s = jnp.einsum('bqngd,bqknd->bqngk', qg, ks).astype(F32) * (HD ** -0.5)
    s = jnp.where(valid[:, :, None, None, :], s, -jnp.inf)
    p = jax.nn.softmax(s, axis=-1).astype(vs.dtype)
    o = jnp.einsum('bqngk,bqknd->bqngd', p, vs)
    return o.reshape(B, Q, H * HD)


def dsa_prompt(q, k, v, qi, ki, wi, topk):
    B, T = q.shape[:2]
    nb = T // Q_BLOCK
    bidx = jnp.arange(B)[:, None, None]
    kpos = jnp.arange(T)

    def blocks(x):
        return jnp.swapaxes(x.reshape((B, nb, Q_BLOCK) + x.shape[2:]), 0, 1)

    def one_block(args):
        qb, qib, wib, t0 = args
        tpos = t0 + jnp.arange(Q_BLOCK)
        visible = kpos[None, :] <= tpos[:, None]
        score = jnp.where(visible[None], index_scores(qib, ki, wib), -jnp.inf)
        _, sel = lax.top_k(score, topk)
        valid = sel <= tpos[None, :, None]
        return sparse_attend(qb, k[bidx, sel], v[bidx, sel], valid)

    out = lax.map(one_block, (blocks(q), blocks(qi), blocks(wi), jnp.arange(nb) * Q_BLOCK))
    return jnp.swapaxes(out, 0, 1).reshape(B, T, ATT_Q_W)


def dsa_sample(q, k_new, v_new, qi, ki_new, wi, cache_k, cache_v, cache_kidx, page_table, layer, topk):
    B, S = q.shape[:2]
    past = page_table.shape[1] * PAGE_SIZE
    slots = jnp.arange(PAGE_SIZE)
    ki_past = cache_kidx[page_table[:, :, None], slots[None, None, :], layer].reshape(B, past, IDX_DIM)
    ki_all = jnp.concatenate([ki_past.astype(ki_new.dtype), ki_new], axis=1)
    qpos = past + jnp.arange(S)
    kpos = jnp.arange(past + S)
    visible = kpos[None, :] <= qpos[:, None]
    score = jnp.where(visible[None], index_scores(qi, ki_all, wi), -jnp.inf)
    _, sel = lax.top_k(score, topk)
    in_past = sel < past
    ps = jnp.minimum(sel, past - 1)
    phys = jnp.take_along_axis(page_table, (ps // PAGE_SIZE).reshape(B, S * topk), axis=1).reshape(B, S, topk)
    slot = ps % PAGE_SIZE
    bidx = jnp.arange(B)[:, None, None]
    ns = jnp.clip(sel - past, 0, S - 1)
    sel_k = jnp.where(in_past[..., None, None], cache_k[phys, slot, layer].astype(k_new.dtype), k_new[bidx, ns])
    sel_v = jnp.where(in_past[..., None, None], cache_v[phys, slot, layer].astype(v_new.dtype), v_new[bidx, ns])
    valid = sel <= qpos[None, :, None]
    return sparse_attend(q, sel_k, sel_v, valid)


def branch_merge(o_dn, o_att, g_dn, g_att, w_dn_up, w_att_up, w_o):
    y = jax.nn.sigmoid(g_dn) * (o_dn @ w_dn_up) + jax.nn.sigmoid(g_att) * (o_att @ w_att_up)
    return y @ w_o


def mem_kv(mem, g, w_mk, w_mv):
    B, M, _ = mem.shape
    m = rms_norm(mem, g)
    return ((m @ w_mk).reshape(B, M, MEM_HEADS, MEM_HEAD_DIM), (m @ w_mv).reshape(B, M, MEM_HEADS, MEM_HEAD_DIM))


def mem_attend(x, mk, mv, g, w_mq, w_mo):
    B, T, _ = x.shape
    q = (rms_norm(x, g) @ w_mq).reshape(B, T, MEM_HEADS, MEM_HEAD_DIM)
    s = jnp.einsum('bthd,bmhd->bhtm', q, mk.astype(q.dtype)).astype(F32) * (MEM_HEAD_DIM ** -0.5)
    p = jax.nn.softmax(s, axis=-1).astype(q.dtype)
    o = jnp.einsum('bhtm,bmhd->bthd', p, mv.astype(q.dtype)).reshape(B, T, MEM_W)
    return o @ w_mo


def peer_block(h, wq, sub_keys, u, v):
    n = h.shape[0]
    q = (h @ wq).reshape(n, PEER_HEADS, 2, PEER_QDIM // 2)
    s = jnp.einsum('nhcd,hckd->nhck', q, sub_keys).astype(F32)
    s_top, i_top = lax.top_k(s, PEER_TOPK)
    cand = (s_top[:, :, 0, :, None] + s_top[:, :, 1, None, :]).reshape(n, PEER_HEADS, PEER_TOPK * PEER_TOPK)
    cidx = (i_top[:, :, 0, :, None] * N_KEYS + i_top[:, :, 1, None, :]).reshape(n, PEER_HEADS, PEER_TOPK * PEER_TOPK)
    best, pos = lax.top_k(cand, PEER_TOPK)
    eidx = jnp.take_along_axis(cidx, pos, axis=-1).reshape(n, PEER_HEADS * PEER_TOPK)
    gate = jax.nn.softmax(best, axis=-1).reshape(n, PEER_HEADS * PEER_TOPK)
    act = jax.nn.gelu(jnp.einsum('nd,ned->ne', h, u[eidx]).astype(F32), approximate=False)
    return jnp.einsum('ne,ned->nd', (gate * act).astype(h.dtype), v[eidx])


def peer_ffn(h, wq, sub_keys, u, v):
    B, T, D = h.shape
    n = B * T
    flat = h.reshape(n, D)
    if n % PEER_BLOCK == 0 and n > PEER_BLOCK:
        out = lax.map(lambda hb: peer_block(hb, wq, sub_keys, u, v), flat.reshape(n // PEER_BLOCK, PEER_BLOCK, D))
        out = out.reshape(n, D)
    else:
        out = peer_block(flat, wq, sub_keys, u, v)
    return out.reshape(B, T, D)


def kernel(x_prompt, x_sample, cache_k, cache_v, cache_kidx, state_conv, state_delta, cache_mem_k, cache_mem_v,
           page_table, mem_prompt, norm_mix, w_in, dn_conv_w, dn_a_log, dn_dt_bias, dn_norm, w_dn_up, w_att_up,
           w_o, norm_mem_q, norm_mem_kv, w_mq, w_mk, w_mv, w_mo, norm_ffn, peer_wq, peer_keys, peer_u, peer_v,
           norm_final):
    B, T, _ = x_prompt.shape
    S = x_sample.shape[1]
    past = page_table.shape[1] * PAGE_SIZE
    pos_p = jnp.arange(T)
    pos_s = past + jnp.arange(S)
    topk_p = min(TOPK_MAX, T // 4)
    topk_s = min(TOPK_MAX, (past + S) // 4)

    hp, hs = x_prompt, x_sample
    p_k, p_v, p_kidx, p_conv, p_delta, p_mk, p_mv = [], [], [], [], [], [], []
    s_k, s_v, s_kidx, s_conv, s_delta = [], [], [], [], []
    for l in range(DEPTH):
        (conv_in, z, a, b, q, k, v, qi, ki, wi, g_dn, g_att) = split_columns(rms_norm(hp, norm_mix[l]) @ w_in[l])
        cbuf = jnp.concatenate([jnp.zeros((B, CONV_W - 1, CONV_DIM), conv_in.dtype), conv_in], axis=1)
        dq, dk, dv, dg, dbeta = dn_features(causal_dwconv_silu(cbuf, dn_conv_w[l]), a, b, dn_a_log[l], dn_dt_bias[l])
        o, st = gated_delta_chunked(dq, dk, dv, dg, dbeta)
        o_dn = dn_output(o, z, dn_norm[l], hp.dtype)
        aq, ak, av, aqi, aki = dsa_features(q, k, v, qi, ki, pos_p)
        o_att = dsa_prompt(aq, ak, av, aqi, aki, wi, topk_p)
        hp = hp + branch_merge(o_dn, o_att, g_dn, g_att, w_dn_up[l], w_att_up[l], w_o[l])
        mk, mv = mem_kv(mem_prompt, norm_mem_kv[l], w_mk[l], w_mv[l])
        hp = hp + mem_attend(hp, mk, mv, norm_mem_q[l], w_mq[l], w_mo[l])
        hp = hp + peer_ffn(rms_norm(hp, norm_ffn[l]), peer_wq[l], peer_keys[l], peer_u[l], peer_v[l])
        p_k.append(ak)
        p_v.append(av)
        p_kidx.append(aki)
        p_conv.append(cbuf[:, -(CONV_W - 1):])
        p_delta.append(st)
        p_mk.append(mk)
        p_mv.append(mv)

        (conv_in, z, a, b, q, k, v, qi, ki, wi, g_dn, g_att) = split_columns(rms_norm(hs, norm_mix[l]) @ w_in[l])
        cbuf = jnp.concatenate([state_conv[l].astype(conv_in.dtype), conv_in], axis=1)
        dq, dk, dv, dg, dbeta = dn_features(causal_dwconv_silu(cbuf, dn_conv_w[l]), a, b, dn_a_log[l], dn_dt_bias[l])
        o, st = gated_delta_recurrent(dq, dk, dv, dg, dbeta, state_delta[l].astype(F32))
        o_dn = dn_output(o, z, dn_norm[l], hs.dtype)
        aq, ak, av, aqi, aki = dsa_features(q, k, v, qi, ki, pos_s)
        o_att = dsa_sample(aq, ak, av, aqi, aki, wi, cache_k, cache_v, cache_kidx, page_table, l, topk_s)
        hs = hs + branch_merge(o_dn, o_att, g_dn, g_att, w_dn_up[l], w_att_up[l], w_o[l])
        hs = hs + mem_attend(hs, cache_mem_k[l], cache_mem_v[l], norm_mem_q[l], w_mq[l], w_mo[l])
        hs = hs + peer_ffn(rms_norm(hs, norm_ffn[l]), peer_wq[l], peer_keys[l], peer_u[l], peer_v[l])
        s_k.append(ak)
        s_v.append(av)
        s_kidx.append(aki)
        s_conv.append(cbuf[:, -(CONV_W - 1):])
        s_delta.append(st)

    y_prompt = pallas_rms_norm(hp.reshape(B * T, D_MODEL), norm_final, 512).reshape(B, T, D_MODEL)
    y_sample = pallas_rms_norm(hs.reshape(-1, D_MODEL), norm_final, 32).reshape(hs.shape)
    return (y_prompt, y_sample,
            jnp.stack(p_k, axis=2), jnp.stack(p_v, axis=2), jnp.stack(p_kidx, axis=2),
            jnp.stack(p_conv, axis=0), jnp.stack(p_delta, axis=0),
            jnp.stack(p_mk, axis=0), jnp.stack(p_mv, axis=0),
            jnp.stack(s_k, axis=2), jnp.stack(s_v, axis=2), jnp.stack(s_kidx, axis=2),
            jnp.stack(s_conv, axis=0), jnp.stack(s_delta, axis=0))
```

```python
import functools

import jax
import jax.numpy as jnp
import numpy as np
from jax import lax
from jax.experimental import pallas as pl
from jax.experimental.pallas import tpu as pltpu

D_MODEL = 1024
DEPTH = 2
PAGE_SIZE = 128
DN_HEADS = 4
DN_DK = 128
DN_DV = 128
CONV_W = 4
DN_CHUNK = 64
ATT_HEADS = 8
KV_HEADS = 2
HEAD_DIM = 64
IDX_HEADS = 4
IDX_DIM = 64
TOPK_MAX = 256
ROPE_THETA = 500000.0
ROT_FRACTION = 4
Q_BLOCK = 128
MEM_LEN = 256
MEM_HEADS = 4
MEM_HEAD_DIM = 128
N_KEYS = 128
N_EXPERTS = N_KEYS * N_KEYS
PEER_HEADS = 8
PEER_QDIM = 256
PEER_TOPK = 16
PEER_BLOCK = 128
NORM_EPS = 1e-6

DN_QK_W = DN_HEADS * DN_DK
DN_V_W = DN_HEADS * DN_DV
CONV_DIM = 2 * DN_QK_W + DN_V_W
ATT_Q_W = ATT_HEADS * HEAD_DIM
ATT_KV_W = KV_HEADS * HEAD_DIM
IDX_Q_W = IDX_HEADS * IDX_DIM
MEM_W = MEM_HEADS * MEM_HEAD_DIM
IN_SPLITS = (CONV_DIM, DN_V_W, DN_HEADS, DN_HEADS, ATT_Q_W, ATT_KV_W, ATT_KV_W, IDX_Q_W, IDX_DIM, IDX_HEADS,
             D_MODEL, D_MODEL)
F32 = jnp.float32


def _rms_kernel(x_ref, g_ref, o_ref):
    x = x_ref[...]
    ms = jnp.mean(x * x, axis=-1, keepdims=True)
    o_ref[...] = x * lax.rsqrt(ms + NORM_EPS) * g_ref[...]


def pallas_rms_norm(x2d, g, tm):
    n, d = x2d.shape
    return pl.pallas_call(
        _rms_kernel,
        grid=(n // tm,),
        in_specs=[pl.BlockSpec((tm, d), lambda i: (i, 0)), pl.BlockSpec((1, d), lambda i: (0, 0))],
        out_specs=pl.BlockSpec((tm, d), lambda i: (i, 0)),
        out_shape=jax.ShapeDtypeStruct((n, d), F32),
        name="rms_norm",
    )(x2d, g.reshape(1, d))


BF16 = jnp.bfloat16
_NT = (((1,), (1,)), ((), ()))
PEER_GATE_TOKENS = 256
PEER_DENSE_TOKENS = 512
PEER_EXPERT_BLOCK = 1024
VMEM_LIMIT = 48 * 1024 * 1024
_STAIR = [(a, b) for a in range(PEER_TOPK) for b in range(PEER_TOPK) if (a + 1) * (b + 1) <= PEER_TOPK]


def _peer_gate_kernel(x_ref, g_ref, wqt_ref, keys_ref, hn_ref, e1_ref, lrow_ref, e2_ref, r2_ref,
                      s_scr, rank_scr, tv_scr):
    x = x_ref[...]
    ms = jnp.mean(x * x, axis=-1, keepdims=True)
    hn = (x * lax.rsqrt(ms + NORM_EPS) * g_ref[...]).astype(BF16)
    hn_ref[...] = hn
    qt = lax.dot_general(wqt_ref[...], hn, _NT, preferred_element_type=F32)
    half = PEER_QDIM // 2
    for p in range(2 * PEER_HEADS):
        qp = qt[p * half:(p + 1) * half, :].astype(BF16)
        s_scr[p] = jnp.dot(keys_ref[p], qp, preferred_element_type=F32)

    def extract(p, carry):
        cur = s_scr[p]
        rank = jnp.full(cur.shape, float(PEER_TOPK), F32)
        tops = []
        for r in range(PEER_TOPK):
            m = jnp.max(cur, axis=0, keepdims=True)
            hit = cur == m
            rank = jnp.where(hit, float(r), rank)
            cur = jnp.where(hit, -jnp.inf, cur)
            tops.append(m)
        rank_scr[p] = rank
        tv_scr[p] = jnp.concatenate(tops, axis=0)
        return carry

    lax.fori_loop(0, 2 * PEER_HEADS, extract, 0)

    def per_head(h, carry):
        tv1 = tv_scr[2 * h]
        tv2 = tv_scr[2 * h + 1]
        m1 = tv1[0:1]
        m2 = tv2[0:1]
        rows = [tv1[a:a + 1] + tv2[b:b + 1] for (a, b) in _STAIR]
        pad = (-len(rows)) % 8
        rows += [jnp.full_like(m1, -jnp.inf)] * pad
        cand = jnp.concatenate(rows, axis=0)
        n_gt = jnp.zeros(cand.shape, F32)
        for r in rows[:len(_STAIR)]:
            n_gt = n_gt + (r > cand).astype(F32)
        thr = jnp.min(jnp.where(n_gt <= float(PEER_TOPK - 1), cand, jnp.inf), axis=0, keepdims=True)
        ex1 = jnp.exp(tv1 - m1)
        ex2 = jnp.exp(tv2 - m2)
        l_rows = []
        z = jnp.zeros_like(m1)
        for a in range(PEER_TOPK):
            sel = (tv1[a:a + 1] + tv2) >= thr
            l_rows.append(jnp.sum(sel.astype(F32), axis=0, keepdims=True))
            z = z + ex1[a:a + 1] * jnp.sum(jnp.where(sel, ex2, 0.0), axis=0, keepdims=True)
        rank1 = rank_scr[2 * h]
        rank2 = rank_scr[2 * h + 1]
        lrow = jnp.zeros(rank1.shape, F32)
        for a in range(PEER_TOPK):
            lrow = jnp.where(rank1 == float(a), l_rows[a], lrow)
        in1 = rank1 < float(PEER_TOPK)
        in2 = rank2 < float(PEER_TOPK)
        e1_ref[h] = jnp.where(in1, jnp.exp(jnp.where(in1, s_scr[2 * h] - m1, 0.0)), 0.0) / z
        lrow_ref[h] = lrow
        e2_ref[h] = jnp.where(in2, jnp.exp(jnp.where(in2, s_scr[2 * h + 1] - m2, 0.0)), 0.0)
        r2_ref[h] = rank2
        return carry

    lax.fori_loop(0, PEER_HEADS, per_head, 0)


def _peer_dense_kernel(hn_ref, u_ref, vt_ref, e1_ref, lrow_ref, e2_ref, r2_ref, hp_ref, o_ref, acc_ref, wt_ref):
    j = pl.program_id(1)

    @pl.when(j == 0)
    def _():
        acc_ref[...] = jnp.zeros_like(acc_ref)

    act = lax.dot_general(u_ref[...], hn_ref[...], _NT, preferred_element_type=F32)
    rows_per_step = PEER_EXPERT_BLOCK // N_KEYS
    for ii in range(rows_per_step):
        a = act[ii * N_KEYS:(ii + 1) * N_KEYS, :]
        ga = 0.5 * a * (1.0 + lax.erf(a * (2.0 ** -0.5)))
        gate = jnp.zeros(a.shape, F32)
        for h in range(PEER_HEADS):
            lr = lrow_ref[h, ii:ii + 1, :]
            e1 = e1_ref[h, ii:ii + 1, :]
            gate = gate + jnp.where(r2_ref[h] < lr, e2_ref[h], 0.0) * e1
        wt_ref[ii * N_KEYS:(ii + 1) * N_KEYS, :] = (gate * ga).astype(BF16)
    acc_ref[...] += jnp.dot(vt_ref[...], wt_ref[...], preferred_element_type=F32)

    @pl.when(j == pl.num_programs(1) - 1)
    def _():
        o_ref[...] = hp_ref[...] + acc_ref[...].T


def peer_residual_pallas(hp2d, g, wq, sub_keys, u, v):
    n, d = hp2d.shape
    tg, td = min(PEER_GATE_TOKENS, n), min(PEER_DENSE_TOKENS, n)
    nq = PEER_HEADS * PEER_QDIM
    wqt = wq.T.astype(BF16)
    keys = sub_keys.reshape(2 * PEER_HEADS, N_KEYS, PEER_QDIM // 2).astype(BF16)
    u_bf = u.astype(BF16)
    vt_bf = v.T.astype(BF16)
    gate_shape = jax.ShapeDtypeStruct((PEER_HEADS, N_KEYS, n), F32)
    gate_spec = pl.BlockSpec((PEER_HEADS, N_KEYS, tg), lambda i: (0, 0, i))
    hn, e1, lrow, e2, r2 = pl.pallas_call(
        _peer_gate_kernel,
        grid=(n // tg,),
        in_specs=[pl.BlockSpec((tg, d), lambda i: (i, 0)),
                  pl.BlockSpec((1, d), lambda i: (0, 0)),
                  pl.BlockSpec((nq, d), lambda i: (0, 0)),
                  pl.BlockSpec((2 * PEER_HEADS, N_KEYS, PEER_QDIM // 2), lambda i: (0, 0, 0))],
        out_specs=[pl.BlockSpec((tg, d), lambda i: (i, 0)), gate_spec, gate_spec, gate_spec, gate_spec],
        out_shape=[jax.ShapeDtypeStruct((n, d), BF16), gate_shape, gate_shape, gate_shape, gate_shape],
        scratch_shapes=[pltpu.VMEM((2 * PEER_HEADS, N_KEYS, tg), F32),
                        pltpu.VMEM((2 * PEER_HEADS, N_KEYS, tg), F32),
                        pltpu.VMEM((2 * PEER_HEADS, PEER_TOPK, tg), F32)],
        compiler_params=pltpu.CompilerParams(dimension_semantics=("arbitrary",), vmem_limit_bytes=VMEM_LIMIT),
        name="peer_gate",
    )(hp2d, g.reshape(1, d), wqt, keys)

    rows = PEER_EXPERT_BLOCK // N_KEYS
    full_spec = pl.BlockSpec((PEER_HEADS, N_KEYS, td), lambda i, j: (0, 0, i))
    row_spec = pl.BlockSpec((PEER_HEADS, rows, td), lambda i, j: (0, j, i))
    return pl.pallas_call(
        _peer_dense_kernel,
        grid=(n // td, N_EXPERTS // PEER_EXPERT_BLOCK),
        in_specs=[pl.BlockSpec((td, d), lambda i, j: (i, 0)),
                  pl.BlockSpec((PEER_EXPERT_BLOCK, d), lambda i, j: (j, 0)),
                  pl.BlockSpec((d, PEER_EXPERT_BLOCK), lambda i, j: (0, j)),
                  row_spec, row_spec, full_spec, full_spec,
                  pl.BlockSpec((td, d), lambda i, j: (i, 0))],
        out_specs=pl.BlockSpec((td, d), lambda i, j: (i, 0)),
        out_shape=jax.ShapeDtypeStruct((n, d), F32),
        scratch_shapes=[pltpu.VMEM((d, td), F32), pltpu.VMEM((PEER_EXPERT_BLOCK, td), BF16)],
        compiler_params=pltpu.CompilerParams(dimension_semantics=("arbitrary", "arbitrary"),
                                             vmem_limit_bytes=VMEM_LIMIT),
        name="peer_dense",
    )(hn, u_bf, vt_bf, e1, lrow, e2, r2, hp2d)


DSA_Q_TILE = 128
DSA_PREFIX_CHUNK = 512


def _dsa_prompt_kernel(qi_ref, ki_ref, wit_ref, q_ref, k_ref, vt_ref, tri_ref, o_ref, *, topk):
    tq = q_ref.shape[1]
    s_len = k_ref.shape[1]
    t0 = pl.program_id(1) * tq
    ki = ki_ref[0].astype(BF16)
    qi = qi_ref[0].astype(BF16)
    score = jnp.zeros((s_len, tq), F32)
    for h in range(IDX_HEADS):
        sh = lax.dot_general(ki, qi[:, h * IDX_DIM:(h + 1) * IDX_DIM], _NT, preferred_element_type=F32)
        score = score + jnp.maximum(sh * (IDX_DIM ** -0.5), 0.0) * (wit_ref[0, h:h + 1, :] * (IDX_HEADS ** -0.5))
    kpos = lax.broadcasted_iota(jnp.int32, (s_len, tq), 0)
    tpos = t0 + lax.broadcasted_iota(jnp.int32, (s_len, tq), 1)
    vis = kpos <= tpos
    score = jnp.where(vis, score + 0.0, -jnp.inf)
    bits = pltpu.bitcast(score, jnp.int32)
    key = bits ^ ((bits >> 31) & jnp.int32(0x7FFFFFFF))

    def search(i, best):
        cand = best + jnp.left_shift(jnp.int32(1), 31 - i)
        cnt = jnp.sum((key >= cand).astype(jnp.int32), axis=0, keepdims=True)
        return jnp.where(cnt >= topk, cand, best)

    kth = lax.fori_loop(0, 32, search, jnp.full((1, tq), jnp.iinfo(jnp.int32).min, jnp.int32))
    above = key > kth
    need = (topk - jnp.sum(above.astype(jnp.int32), axis=0, keepdims=True)).astype(F32)
    tie = key == kth
    tie_bf = tie.astype(BF16)
    carry = jnp.zeros((1, tq), F32)
    pre = []
    for c in range(s_len // DSA_PREFIX_CHUNK):
        blk = tie_bf[c * DSA_PREFIX_CHUNK:(c + 1) * DSA_PREFIX_CHUNK, :]
        inc = jnp.dot(tri_ref[...], blk, preferred_element_type=F32) + carry
        pre.append(inc)
        carry = inc[DSA_PREFIX_CHUNK - 1:DSA_PREFIX_CHUNK, :]
    prefix = jnp.concatenate(pre, axis=0)
    sel = vis & (above | (tie & (prefix <= need)))

    outs = []
    group = ATT_HEADS // KV_HEADS
    for g in range(KV_HEADS):
        kg = k_ref[0][:, g * HEAD_DIM:(g + 1) * HEAD_DIM].astype(BF16)
        vtg = vt_ref[0][g * HEAD_DIM:(g + 1) * HEAD_DIM, :].astype(BF16)
        for hh in range(group):
            h = g * group + hh
            qh = q_ref[0][:, h * HEAD_DIM:(h + 1) * HEAD_DIM].astype(BF16)
            s = lax.dot_general(kg, qh, _NT, preferred_element_type=F32) * (HEAD_DIM ** -0.5)
            s = jnp.where(sel, s, -jnp.inf)
            m = jnp.max(s, axis=0, keepdims=True)
            p = jnp.exp(s - m)
            denom = jnp.sum(p, axis=0, keepdims=True)
            outs.append(jnp.dot(vtg, p.astype(BF16), preferred_element_type=F32) / denom)
    o_ref[0] = jnp.concatenate(outs, axis=0).T


def dsa_prompt_pallas(q, k, v, qi, ki, wi, topk):
    B, T = q.shape[:2]
    tq = min(DSA_Q_TILE, T)
    chunk = min(DSA_PREFIX_CHUNK, T)
    assert T % tq == 0 and T % chunk == 0
    q2 = q.reshape(B, T, ATT_Q_W)
    k2 = k.reshape(B, T, ATT_KV_W)
    vt = jnp.swapaxes(v.reshape(B, T, ATT_KV_W), 1, 2)
    qi2 = qi.reshape(B, T, IDX_Q_W)
    wit = jnp.swapaxes(wi, 1, 2)
    tri = jnp.tril(jnp.ones((chunk, chunk), BF16))
    kern = functools.partial(_dsa_prompt_kernel, topk=topk)
    return pl.pallas_call(
        kern,
        grid=(B, T // tq),
        in_specs=[pl.BlockSpec((1, tq, IDX_Q_W), lambda b, i: (b, i, 0)),
                  pl.BlockSpec((1, T, IDX_DIM), lambda b, i: (b, 0, 0)),
                  pl.BlockSpec((1, IDX_HEADS, tq), lambda b, i: (b, 0, i)),
                  pl.BlockSpec((1, tq, ATT_Q_W), lambda b, i: (b, i, 0)),
                  pl.BlockSpec((1, T, ATT_KV_W), lambda b, i: (b, 0, 0)),
                  pl.BlockSpec((1, ATT_KV_W, T), lambda b, i: (b, 0, 0)),
                  pl.BlockSpec((chunk, chunk), lambda b, i: (0, 0))],
        out_specs=pl.BlockSpec((1, tq, ATT_Q_W), lambda b, i: (b, i, 0)),
        out_shape=jax.ShapeDtypeStruct((B, T, ATT_Q_W), F32),
        compiler_params=pltpu.CompilerParams(dimension_semantics=("arbitrary", "arbitrary"),
                                             vmem_limit_bytes=VMEM_LIMIT),
        name="dsa_prompt",
    )(qi2, ki, wit, q2, k2, vt, tri)


def rms_norm(x, g):
    xf = x.astype(F32)
    y = xf * lax.rsqrt(jnp.mean(xf * xf, axis=-1, keepdims=True) + NORM_EPS)
    return (y * g.astype(F32)).astype(x.dtype)


def l2_normalize(x):
    return x * lax.rsqrt(jnp.sum(x * x, axis=-1, keepdims=True) + NORM_EPS)


def rope_partial(x, pos):
    rot = x.shape[-1] // ROT_FRACTION
    half = rot // 2
    inv_freq = ROPE_THETA ** (-jnp.arange(half, dtype=F32) / half)
    ang = pos.astype(F32)[:, None] * inv_freq[None, :]
    cos = jnp.cos(ang)[:, None, :]
    sin = jnp.sin(ang)[:, None, :]
    xr = x[..., :rot].astype(F32)
    x1, x2 = xr[..., :half], xr[..., half:]
    rotated = jnp.concatenate([x1 * cos - x2 * sin, x2 * cos + x1 * sin], axis=-1)
    return jnp.concatenate([rotated.astype(x.dtype), x[..., rot:]], axis=-1)


def split_columns(z):
    offsets = np.cumsum(IN_SPLITS)[:-1].tolist()
    return jnp.split(z, offsets, axis=-1)


def causal_dwconv_silu(xp, w):
    y = lax.conv_general_dilated(xp, w[:, None, :].astype(xp.dtype), window_strides=(1,), padding='VALID',
                                 dimension_numbers=('NWC', 'WIO', 'NWC'), feature_group_count=xp.shape[-1])
    return jax.nn.silu(y)


def dn_features(c, a, b, a_log, dt_bias):
    B, T, _ = c.shape
    cf = c.astype(F32)
    q = l2_normalize(cf[..., :DN_QK_W].reshape(B, T, DN_HEADS, DN_DK)) * (DN_DK ** -0.5)
    k = l2_normalize(cf[..., DN_QK_W:2 * DN_QK_W].reshape(B, T, DN_HEADS, DN_DK))
    v = cf[..., 2 * DN_QK_W:].reshape(B, T, DN_HEADS, DN_DV)
    beta = jax.nn.sigmoid(b.astype(F32))
    g = -jnp.exp(a_log.astype(F32)) * jax.nn.softplus(a.astype(F32) + dt_bias.astype(F32))
    return q, k, v, g, beta


def dn_output(o, z, gain, dtype):
    B, T = o.shape[:2]
    on = o * lax.rsqrt(jnp.mean(o * o, axis=-1, keepdims=True) + NORM_EPS) * gain.astype(F32)
    y = on * jax.nn.silu(z.astype(F32)).reshape(B, T, DN_HEADS, DN_DV)
    return y.reshape(B, T, DN_V_W).astype(dtype)


def gated_delta_chunked(q, k, v, g, beta):
    B, T, H, DK = q.shape
    DV = v.shape[-1]
    C = DN_CHUNK
    NC = T // C

    def chunks(x):
        return jnp.moveaxis(x.reshape((B, NC, C, H) + x.shape[3:]), 3, 1)

    qc, kc, vc, bc = chunks(q), chunks(k), chunks(v), chunks(beta)
    gc = jnp.cumsum(chunks(g), axis=-1)
    tri = jnp.tril(jnp.ones((C, C), bool))
    strict = jnp.tril(jnp.ones((C, C), bool), -1)
    diff = gc[..., :, None] - gc[..., None, :]
    decay = jnp.where(tri, jnp.exp(jnp.where(tri, diff, 0.0)), 0.0)
    kk = jnp.einsum('bhnid,bhnjd->bhnij', kc, kc)
    a_mat = jnp.where(strict, bc[..., :, None] * kk * decay, 0.0) + jnp.eye(C, dtype=F32)
    rhs = jnp.concatenate([vc * bc[..., None], kc * (bc * jnp.exp(gc))[..., None]], axis=-1)
    sol = lax.linalg.triangular_solve(a_mat, rhs, left_side=True, lower=True)
    val, kcd = sol[..., :DV], sol[..., DV:]
    a_qk = jnp.einsum('bhnid,bhnjd->bhnij', qc, kc) * decay
    q_dec = qc * jnp.exp(gc)[..., None]
    g_last = gc[..., -1]
    k_dec = kc * jnp.exp(g_last[..., None] - gc)[..., None]

    def step(S, xs):
        qd, kd, vl, kcd_c, aqk, gl = xs
        v_new = vl - jnp.einsum('bhck,bhkv->bhcv', kcd_c, S)
        o = jnp.einsum('bhck,bhkv->bhcv', qd, S) + jnp.einsum('bhcj,bhjv->bhcv', aqk, v_new)
        S = S * jnp.exp(gl)[..., None, None] + jnp.einsum('bhck,bhcv->bhkv', kd, v_new)
        return S, o

    xs = tuple(jnp.moveaxis(t, 2, 0) for t in (q_dec, k_dec, val, kcd, a_qk, g_last))
    S_fin, o = lax.scan(step, jnp.zeros((B, H, DK, DV), F32), xs)
    o = jnp.transpose(o, (1, 0, 3, 2, 4)).reshape(B, T, H, DV)
    return o, S_fin


def gated_delta_recurrent(q, k, v, g, beta, S0):
    def step(S, xs):
        qt, kt, vt, gt, bt = xs
        S = S * jnp.exp(gt)[..., None, None]
        pred = jnp.einsum('bhk,bhkv->bhv', kt, S)
        S = S + jnp.einsum('bhk,bhv->bhkv', kt, bt[..., None] * (vt - pred))
        return S, jnp.einsum('bhk,bhkv->bhv', qt, S)

    xs = tuple(jnp.moveaxis(t, 1, 0) for t in (q, k, v, g, beta))
    S_fin, o = lax.scan(step, S0, xs)
    return jnp.moveaxis(o, 0, 1), S_fin


def dsa_features(q, k, v, qi, ki, pos):
    B, T, _ = q.shape
    q = rope_partial(q.reshape(B, T, ATT_HEADS, HEAD_DIM), pos)
    k = rope_partial(k.reshape(B, T, KV_HEADS, HEAD_DIM), pos)
    v = v.reshape(B, T, KV_HEADS, HEAD_DIM)
    qi = rope_partial(qi.reshape(B, T, IDX_HEADS, IDX_DIM), pos)
    ki = rope_partial(ki.reshape(B, T, 1, IDX_DIM), pos)[:, :, 0]
    return q, k, v, qi, ki


def index_scores(qi, ki, wi):
    s = jax.nn.relu(jnp.einsum('bqhd,bsd->bqhs', qi, ki).astype(F32) * (IDX_DIM ** -0.5))
    return jnp.einsum('bqh,bqhs->bqs', wi.astype(F32) * (IDX_HEADS ** -0.5), s)


def sparse_attend(q, ks, vs, valid):
    B, Q, H, HD = q.shape
    qg = q.reshape(B, Q, KV_HEADS, H // KV_HEADS, HD)
    s = jnp.einsum('bqngd,bqknd->bqngk', qg, ks).astype(F32) * (HD ** -0.5)
    s = jnp.where(valid[:, :, None, None, :], s, -jnp.inf)
    p = jax.nn.softmax(s, axis=-1).astype(vs.dtype)
    o = jnp.einsum('bqngk,bqknd->bqngd', p, vs)
    return o.reshape(B, Q, H * HD)


def dsa_prompt(q, k, v, qi, ki, wi, topk):
    B, T = q.shape[:2]
    nb = T // Q_BLOCK
    bidx = jnp.arange(B)[:, None, None]
    kpos = jnp.arange(T)

    def blocks(x):
        return jnp.swapaxes(x.reshape((B, nb, Q_BLOCK) + x.shape[2:]), 0, 1)

    def one_block(args):
        qb, qib, wib, t0 = args
        tpos = t0 + jnp.arange(Q_BLOCK)
        visible = kpos[None, :] <= tpos[:, None]
        score = jnp.where(visible[None], index_scores(qib, ki, wib), -jnp.inf)
        _, sel = lax.top_k(score, topk)
        valid = sel <= tpos[None, :, None]
        return sparse_attend(qb, k[bidx, sel], v[bidx, sel], valid)

    out = lax.map(one_block, (blocks(q), blocks(qi), blocks(wi), jnp.arange(nb) * Q_BLOCK))
    return jnp.swapaxes(out, 0, 1).reshape(B, T, ATT_Q_W)


def dsa_sample(q, k_new, v_new, qi, ki_new, wi, cache_k, cache_v, cache_kidx, page_table, layer, topk):
    B, S = q.shape[:2]
    past = page_table.shape[1] * PAGE_SIZE
    slots = jnp.arange(PAGE_SIZE)
    ki_past = cache_kidx[page_table[:, :, None], slots[None, None, :], layer].reshape(B, past, IDX_DIM)
    ki_all = jnp.concatenate([ki_past.astype(ki_new.dtype), ki_new], axis=1)
    qpos = past + jnp.arange(S)
    kpos = jnp.arange(past + S)
    visible = kpos[None, :] <= qpos[:, None]
    score = jnp.where(visible[None], index_scores(qi, ki_all, wi), -jnp.inf)
    _, sel = lax.top_k(score, topk)
    in_past = sel < past
    ps = jnp.minimum(sel, past - 1)
    phys = jnp.take_along_axis(page_table, (ps // PAGE_SIZE).reshape(B, S * topk), axis=1).reshape(B, S, topk)
    slot = ps % PAGE_SIZE
    bidx = jnp.arange(B)[:, None, None]
    ns = jnp.clip(sel - past, 0, S - 1)
    sel_k = jnp.where(in_past[..., None, None], cache_k[phys, slot, layer].astype(k_new.dtype), k_new[bidx, ns])
    sel_v = jnp.where(in_past[..., None, None], cache_v[phys, slot, layer].astype(v_new.dtype), v_new[bidx, ns])
    valid = sel <= qpos[None, :, None]
    return sparse_attend(q, sel_k, sel_v, valid)


def branch_merge(o_dn, o_att, g_dn, g_att, w_dn_up, w_att_up, w_o):
    y = jax.nn.sigmoid(g_dn) * (o_dn @ w_dn_up) + jax.nn.sigmoid(g_att) * (o_att @ w_att_up)
    return y @ w_o


def mem_kv(mem, g, w_mk, w_mv):
    B, M, _ = mem.shape
    m = rms_norm(mem, g)
    return ((m @ w_mk).reshape(B, M, MEM_HEADS, MEM_HEAD_DIM), (m @ w_mv).reshape(B, M, MEM_HEADS, MEM_HEAD_DIM))


def mem_attend(x, mk, mv, g, w_mq, w_mo):
    B, T, _ = x.shape
    q = (rms_norm(x, g) @ w_mq).reshape(B, T, MEM_HEADS, MEM_HEAD_DIM)
    s = jnp.einsum('bthd,bmhd->bhtm', q, mk.astype(q.dtype)).astype(F32) * (MEM_HEAD_DIM ** -0.5)
    p = jax.nn.softmax(s, axis=-1).astype(q.dtype)
    o = jnp.einsum('bhtm,bmhd->bthd', p, mv.astype(q.dtype)).reshape(B, T, MEM_W)
    return o @ w_mo


def peer_block(h, wq, sub_keys, u, v):
    n = h.shape[0]
    q = (h @ wq).reshape(n, PEER_HEADS, 2, PEER_QDIM // 2)
    s = jnp.einsum('nhcd,hckd->nhck', q, sub_keys).astype(F32)
    s_top, i_top = lax.top_k(s, PEER_TOPK)
    cand = (s_top[:, :, 0, :, None] + s_top[:, :, 1, None, :]).reshape(n, PEER_HEADS, PEER_TOPK * PEER_TOPK)
    cidx = (i_top[:, :, 0, :, None] * N_KEYS + i_top[:, :, 1, None, :]).reshape(n, PEER_HEADS, PEER_TOPK * PEER_TOPK)
    best, pos = lax.top_k(cand, PEER_TOPK)
    eidx = jnp.take_along_axis(cidx, pos, axis=-1).reshape(n, PEER_HEADS * PEER_TOPK)
    gate = jax.nn.softmax(best, axis=-1).reshape(n, PEER_HEADS * PEER_TOPK)
    act = jax.nn.gelu(jnp.einsum('nd,ned->ne', h, u[eidx]).astype(F32), approximate=False)
    return jnp.einsum('ne,ned->nd', (gate * act).astype(h.dtype), v[eidx])


def peer_ffn(h, wq, sub_keys, u, v):
    B, T, D = h.shape
    n = B * T
    flat = h.reshape(n, D)
    if n % PEER_BLOCK == 0 and n > PEER_BLOCK:
        out = lax.map(lambda hb: peer_block(hb, wq, sub_keys, u, v), flat.reshape(n // PEER_BLOCK, PEER_BLOCK, D))
        out = out.reshape(n, D)
    else:
        out = peer_block(flat, wq, sub_keys, u, v)
    return out.reshape(B, T, D)


def kernel(x_prompt, x_sample, cache_k, cache_v, cache_kidx, state_conv, state_delta, cache_mem_k, cache_mem_v,
           page_table, mem_prompt, norm_mix, w_in, dn_conv_w, dn_a_log, dn_dt_bias, dn_norm, w_dn_up, w_att_up,
           w_o, norm_mem_q, norm_mem_kv, w_mq, w_mk, w_mv, w_mo, norm_ffn, peer_wq, peer_keys, peer_u, peer_v,
           norm_final):
    B, T, _ = x_prompt.shape
    S = x_sample.shape[1]
    past = page_table.shape[1] * PAGE_SIZE
    pos_p = jnp.arange(T)
    pos_s = past + jnp.arange(S)
    topk_p = min(TOPK_MAX, T // 4)
    topk_s = min(TOPK_MAX, (past + S) // 4)

    hp, hs = x_prompt, x_sample
    p_k, p_v, p_kidx, p_conv, p_delta, p_mk, p_mv = [], [], [], [], [], [], []
    s_k, s_v, s_kidx, s_conv, s_delta = [], [], [], [], []
    for l in range(DEPTH):
        (conv_in, z, a, b, q, k, v, qi, ki, wi, g_dn, g_att) = split_columns(rms_norm(hp, norm_mix[l]) @ w_in[l])
        cbuf = jnp.concatenate([jnp.zeros((B, CONV_W - 1, CONV_DIM), conv_in.dtype), conv_in], axis=1)
        dq, dk, dv, dg, dbeta = dn_features(causal_dwconv_silu(cbuf, dn_conv_w[l]), a, b, dn_a_log[l], dn_dt_bias[l])
        o, st = gated_delta_chunked(dq, dk, dv, dg, dbeta)
        o_dn = dn_output(o, z, dn_norm[l], hp.dtype)
        aq, ak, av, aqi, aki = dsa_features(q, k, v, qi, ki, pos_p)
        o_att = dsa_prompt_pallas(aq, ak, av, aqi, aki, wi, topk_p)
        hp = hp + branch_merge(o_dn, o_att, g_dn, g_att, w_dn_up[l], w_att_up[l], w_o[l])
        mk, mv = mem_kv(mem_prompt, norm_mem_kv[l], w_mk[l], w_mv[l])
        hp = hp + mem_attend(hp, mk, mv, norm_mem_q[l], w_mq[l], w_mo[l])
        hp = peer_residual_pallas(hp.reshape(B * T, D_MODEL), norm_ffn[l], peer_wq[l], peer_keys[l], peer_u[l],
                                  peer_v[l]).reshape(B, T, D_MODEL)
        p_k.append(ak)
        p_v.append(av)
        p_kidx.append(aki)
        p_conv.append(cbuf[:, -(CONV_W - 1):])
        p_delta.append(st)
        p_mk.append(mk)
        p_mv.append(mv)

        (conv_in, z, a, b, q, k, v, qi, ki, wi, g_dn, g_att) = split_columns(rms_norm(hs, norm_mix[l]) @ w_in[l])
        cbuf = jnp.concatenate([state_conv[l].astype(conv_in.dtype), conv_in], axis=1)
        dq, dk, dv, dg, dbeta = dn_features(causal_dwconv_silu(cbuf, dn_conv_w[l]), a, b, dn_a_log[l], dn_dt_bias[l])
        o, st = gated_delta_recurrent(dq, dk, dv, dg, dbeta, state_delta[l].astype(F32))
        o_dn = dn_output(o, z, dn_norm[l], hs.dtype)
        aq, ak, av, aqi, aki = dsa_features(q, k, v, qi, ki, pos_s)
        o_att = dsa_sample(aq, ak, av, aqi, aki, wi, cache_k, cache_v, cache_kidx, page_table, l, topk_s)
        hs = hs + branch_merge(o_dn, o_att, g_dn, g_att, w_dn_up[l], w_att_up[l], w_o[l])
        hs = hs + mem_attend(hs, cache_mem_k[l], cache_mem_v[l], norm_mem_q[l], w_mq[l], w_mo[l])
        hs = hs + peer_ffn(rms_norm(hs, norm_ffn[l]), peer_wq[l], peer_keys[l], peer_u[l], peer_v[l])
        s_k.append(ak)
        s_v.append(av)
        s_kidx.append(aki)
        s_conv.append(cbuf[:, -(CONV_W - 1):])
        s_delta.append(st)

    y_prompt = pallas_rms_norm(hp.reshape(B * T, D_MODEL), norm_final, 512).reshape(B, T, D_MODEL)
    y_sample = pallas_rms_norm(hs.reshape(-1, D_MODEL), norm_final, 32).reshape(hs.shape)
    return (y_prompt, y_sample,
            jnp.stack(p_k, axis=2), jnp.stack(p_v, axis=2), jnp.stack(p_kidx, axis=2),
            jnp.stack(p_conv, axis=0), jnp.stack(p_delta, axis=0),
            jnp.stack(p_mk, axis=0), jnp.stack(p_mv, axis=0),
            jnp.stack(s_k, axis=2), jnp.stack(s_v, axis=2), jnp.stack(s_kidx, axis=2),
            jnp.stack(s_conv, axis=0), jnp.stack(s_delta, axis=0))
```

```python
import functools

import jax
import jax.numpy as jnp
import numpy as np
from jax import lax
from jax.experimental import pallas as pl
from jax.experimental.pallas import tpu as pltpu

D_MODEL = 1024
DEPTH = 2
PAGE_SIZE = 128
DN_HEADS = 4
DN_DK = 128
DN_DV = 128
CONV_W = 4
DN_CHUNK = 64
ATT_HEADS = 8
KV_HEADS = 2
HEAD_DIM = 64
IDX_HEADS = 4
IDX_DIM = 64
TOPK_MAX = 256
ROPE_THETA = 500000.0
ROT_FRACTION = 4
Q_BLOCK = 128
MEM_LEN = 256
MEM_HEADS = 4
MEM_HEAD_DIM = 128
N_KEYS = 128
N_EXPERTS = N_KEYS * N_KEYS
PEER_HEADS = 8
PEER_QDIM = 256
PEER_TOPK = 16
PEER_BLOCK = 128
NORM_EPS = 1e-6

DN_QK_W = DN_HEADS * DN_DK
DN_V_W = DN_HEADS * DN_DV
CONV_DIM = 2 * DN_QK_W + DN_V_W
ATT_Q_W = ATT_HEADS * HEAD_DIM
ATT_KV_W = KV_HEADS * HEAD_DIM
IDX_Q_W = IDX_HEADS * IDX_DIM
MEM_W = MEM_HEADS * MEM_HEAD_DIM
IN_SPLITS = (CONV_DIM, DN_V_W, DN_HEADS, DN_HEADS, ATT_Q_W, ATT_KV_W, ATT_KV_W, IDX_Q_W, IDX_DIM, IDX_HEADS,
             D_MODEL, D_MODEL)
F32 = jnp.float32


def _rms_kernel(x_ref, g_ref, o_ref):
    x = x_ref[...]
    ms = jnp.mean(x * x, axis=-1, keepdims=True)
    o_ref[...] = x * lax.rsqrt(ms + NORM_EPS) * g_ref[...]


def pallas_rms_norm(x2d, g, tm):
    n, d = x2d.shape
    return pl.pallas_call(
        _rms_kernel,
        grid=(n // tm,),
        in_specs=[pl.BlockSpec((tm, d), lambda i: (i, 0)), pl.BlockSpec((1, d), lambda i: (0, 0))],
        out_specs=pl.BlockSpec((tm, d), lambda i: (i, 0)),
        out_shape=jax.ShapeDtypeStruct((n, d), F32),
        name="rms_norm",
    )(x2d, g.reshape(1, d))


BF16 = jnp.bfloat16
_NT = (((1,), (1,)), ((), ()))
PEER_GATE_TOKENS = 256
PEER_DENSE_TOKENS = 512
PEER_EXPERT_BLOCK = 1024
VMEM_LIMIT = 48 * 1024 * 1024
_STAIR = [(a, b) for a in range(PEER_TOPK) for b in range(PEER_TOPK) if (a + 1) * (b + 1) <= PEER_TOPK]


def _peer_gate_kernel(x_ref, g_ref, wqt_ref, keys_ref, hn_ref, e1_ref, lrow_ref, e2_ref, r2_ref,
                      s_scr, rank_scr, tv_scr):
    x = x_ref[...]
    ms = jnp.mean(x * x, axis=-1, keepdims=True)
    hn = (x * lax.rsqrt(ms + NORM_EPS) * g_ref[...]).astype(BF16)
    hn_ref[...] = hn
    qt = lax.dot_general(wqt_ref[...], hn, _NT, preferred_element_type=F32)
    half = PEER_QDIM // 2
    for p in range(2 * PEER_HEADS):
        qp = qt[p * half:(p + 1) * half, :].astype(BF16)
        s_scr[p] = jnp.dot(keys_ref[p], qp, preferred_element_type=F32)

    def extract(p, carry):
        cur = s_scr[p]
        rank = jnp.full(cur.shape, float(PEER_TOPK), F32)
        tops = []
        for r in range(PEER_TOPK):
            m = jnp.max(cur, axis=0, keepdims=True)
            hit = cur == m
            rank = jnp.where(hit, float(r), rank)
            cur = jnp.where(hit, -jnp.inf, cur)
            tops.append(m)
        rank_scr[p] = rank
        tv_scr[p] = jnp.concatenate(tops, axis=0)
        return carry

    lax.fori_loop(0, 2 * PEER_HEADS, extract, 0)

    def per_head(h, carry):
        tv1 = tv_scr[2 * h]
        tv2 = tv_scr[2 * h + 1]
        m1 = tv1[0:1]
        m2 = tv2[0:1]
        rows = [tv1[a:a + 1] + tv2[b:b + 1] for (a, b) in _STAIR]
        pad = (-len(rows)) % 8
        rows += [jnp.full_like(m1, -jnp.inf)] * pad
        cand = jnp.concatenate(rows, axis=0)
        n_gt = jnp.zeros(cand.shape, F32)
        for r in rows[:len(_STAIR)]:
            n_gt = n_gt + (r > cand).astype(F32)
        thr = jnp.min(jnp.where(n_gt <= float(PEER_TOPK - 1), cand, jnp.inf), axis=0, keepdims=True)
        ex1 = jnp.exp(tv1 - m1)
        ex2 = jnp.exp(tv2 - m2)
        l_rows = []
        z = jnp.zeros_like(m1)
        for a in range(PEER_TOPK):
            sel = (tv1[a:a + 1] + tv2) >= thr
            l_rows.append(jnp.sum(sel.astype(F32), axis=0, keepdims=True))
            z = z + ex1[a:a + 1] * jnp.sum(jnp.where(sel, ex2, 0.0), axis=0, keepdims=True)
        rank1 = rank_scr[2 * h]
        rank2 = rank_scr[2 * h + 1]
        lrow = jnp.zeros(rank1.shape, F32)
        for a in range(PEER_TOPK):
            lrow = jnp.where(rank1 == float(a), l_rows[a], lrow)
        in1 = rank1 < float(PEER_TOPK)
        in2 = rank2 < float(PEER_TOPK)
        e1_ref[h] = jnp.where(in1, jnp.exp(jnp.where(in1, s_scr[2 * h] - m1, 0.0)), 0.0) / z
        lrow_ref[h] = lrow
        e2_ref[h] = jnp.where(in2, jnp.exp(jnp.where(in2, s_scr[2 * h + 1] - m2, 0.0)), 0.0).astype(BF16)
        r2_ref[h] = rank2.astype(BF16)
        return carry

    lax.fori_loop(0, PEER_HEADS, per_head, 0)


def _peer_dense_kernel(hn_ref, u_ref, vt_ref, e1_ref, lrow_ref, e2_ref, r2_ref, hp_ref, o_ref, acc_ref, wt_ref):
    j = pl.program_id(1)

    @pl.when(j == 0)
    def _():
        acc_ref[...] = jnp.zeros_like(acc_ref)

    act = lax.dot_general(u_ref[...], hn_ref[...], _NT, preferred_element_type=F32)
    rows_per_step = PEER_EXPERT_BLOCK // N_KEYS
    for ii in range(rows_per_step):
        a = act[ii * N_KEYS:(ii + 1) * N_KEYS, :]
        ga = 0.5 * a * (1.0 + lax.erf(a * (2.0 ** -0.5)))
        gate = jnp.zeros(a.shape, BF16)
        zero = jnp.zeros(a.shape, BF16)
        for h in range(PEER_HEADS):
            lr = lrow_ref[h, ii:ii + 1, :].astype(BF16)
            e1 = e1_ref[h, ii:ii + 1, :].astype(BF16)
            gate = gate + jnp.where(r2_ref[h] < lr, e2_ref[h], zero) * e1
        wt_ref[ii * N_KEYS:(ii + 1) * N_KEYS, :] = gate * ga.astype(BF16)
    acc_ref[...] += jnp.dot(vt_ref[...], wt_ref[...], preferred_element_type=F32)

    @pl.when(j == pl.num_programs(1) - 1)
    def _():
        o_ref[...] = hp_ref[...] + acc_ref[...].T


def peer_prepare(wq, sub_keys, u, v):
    wqt = wq.T.astype(BF16)
    keys = sub_keys.reshape(2 * PEER_HEADS, N_KEYS, PEER_QDIM // 2).astype(BF16)
    return wqt, keys, u.astype(BF16), v.T.astype(BF16)


def peer_residual_pallas(hp2d, g, prepared):
    n, d = hp2d.shape
    tg, td = min(PEER_GATE_TOKENS, n), min(PEER_DENSE_TOKENS, n)
    nq = PEER_HEADS * PEER_QDIM
    wqt, keys, u_bf, vt_bf = prepared
    gate_shape = jax.ShapeDtypeStruct((PEER_HEADS, N_KEYS, n), F32)
    gate_shape16 = jax.ShapeDtypeStruct((PEER_HEADS, N_KEYS, n), BF16)
    gate_spec = pl.BlockSpec((PEER_HEADS, N_KEYS, tg), lambda i: (0, 0, i))
    hn, e1, lrow, e2, r2 = pl.pallas_call(
        _peer_gate_kernel,
        grid=(n // tg,),
        in_specs=[pl.BlockSpec((tg, d), lambda i: (i, 0)),
                  pl.BlockSpec((1, d), lambda i: (0, 0)),
                  pl.BlockSpec((nq, d), lambda i: (0, 0)),
                  pl.BlockSpec((2 * PEER_HEADS, N_KEYS, PEER_QDIM // 2), lambda i: (0, 0, 0))],
        out_specs=[pl.BlockSpec((tg, d), lambda i: (i, 0)), gate_spec, gate_spec, gate_spec, gate_spec],
        out_shape=[jax.ShapeDtypeStruct((n, d), BF16), gate_shape, gate_shape, gate_shape16, gate_shape16],
        scratch_shapes=[pltpu.VMEM((2 * PEER_HEADS, N_KEYS, tg), F32),
                        pltpu.VMEM((2 * PEER_HEADS, N_KEYS, tg), F32),
                        pltpu.VMEM((2 * PEER_HEADS, PEER_TOPK, tg), F32)],
        compiler_params=pltpu.CompilerParams(dimension_semantics=("arbitrary",), vmem_limit_bytes=VMEM_LIMIT),
        name="peer_gate",
    )(hp2d, g.reshape(1, d), wqt, keys)

    rows = PEER_EXPERT_BLOCK // N_KEYS
    full_spec = pl.BlockSpec((PEER_HEADS, N_KEYS, td), lambda i, j: (0, 0, i))
    row_spec = pl.BlockSpec((PEER_HEADS, rows, td), lambda i, j: (0, j, i))
    return pl.pallas_call(
        _peer_dense_kernel,
        grid=(n // td, N_EXPERTS // PEER_EXPERT_BLOCK),
        in_specs=[pl.BlockSpec((td, d), lambda i, j: (i, 0)),
                  pl.BlockSpec((PEER_EXPERT_BLOCK, d), lambda i, j: (j, 0)),
                  pl.BlockSpec((d, PEER_EXPERT_BLOCK), lambda i, j: (0, j)),
                  row_spec, row_spec, full_spec, full_spec,
                  pl.BlockSpec((td, d), lambda i, j: (i, 0))],
        out_specs=pl.BlockSpec((td, d), lambda i, j: (i, 0)),
        out_shape=jax.ShapeDtypeStruct((n, d), F32),
        scratch_shapes=[pltpu.VMEM((d, td), F32), pltpu.VMEM((PEER_EXPERT_BLOCK, td), BF16)],
        compiler_params=pltpu.CompilerParams(dimension_semantics=("arbitrary", "arbitrary"),
                                             vmem_limit_bytes=VMEM_LIMIT),
        name="peer_dense",
    )(hn, u_bf, vt_bf, e1, lrow, e2, r2, hp2d)


DSA_Q_TILE = 128
DSA_PREFIX_CHUNK = 512
DSA_GROUPS = 4


def _dsa_prompt_kernel(qi_ref, ki_ref, wit_ref, q_ref, k_ref, vt_ref, tri_ref, o_ref, *, topk, q_start):
    tq = q_ref.shape[1]
    s_len = k_ref.shape[1]
    t0 = q_start + pl.program_id(1) * tq
    ki = ki_ref[0].astype(BF16)
    qi = qi_ref[0].astype(BF16)
    score = jnp.zeros((s_len, tq), F32)
    for h in range(IDX_HEADS):
        sh = lax.dot_general(ki, qi[:, h * IDX_DIM:(h + 1) * IDX_DIM], _NT, preferred_element_type=F32)
        score = score + jnp.maximum(sh * (IDX_DIM ** -0.5), 0.0) * (wit_ref[0, h:h + 1, :] * (IDX_HEADS ** -0.5))
    kpos = lax.broadcasted_iota(jnp.int32, (s_len, tq), 0)
    tpos = t0 + lax.broadcasted_iota(jnp.int32, (s_len, tq), 1)
    vis = kpos <= tpos
    score = jnp.where(vis, score + 0.0, -jnp.inf)
    bits = pltpu.bitcast(score, jnp.int32)
    key = bits ^ ((bits >> 31) & jnp.int32(0x7FFFFFFF))

    def search(i, best):
        cand = best + jnp.left_shift(jnp.int32(1), 31 - i)
        cnt = jnp.sum((key >= cand).astype(jnp.int32), axis=0, keepdims=True)
        return jnp.where(cnt >= topk, cand, best)

    kth = lax.fori_loop(0, 32, search, jnp.full((1, tq), jnp.iinfo(jnp.int32).min, jnp.int32))
    above = key > kth
    need = (topk - jnp.sum(above.astype(jnp.int32), axis=0, keepdims=True)).astype(F32)
    tie = key == kth
    tie_bf = tie.astype(BF16)
    carry = jnp.zeros((1, tq), F32)
    pre = []
    for c in range(s_len // DSA_PREFIX_CHUNK):
        blk = tie_bf[c * DSA_PREFIX_CHUNK:(c + 1) * DSA_PREFIX_CHUNK, :]
        inc = jnp.dot(tri_ref[...], blk, preferred_element_type=F32) + carry
        pre.append(inc)
        carry = inc[DSA_PREFIX_CHUNK - 1:DSA_PREFIX_CHUNK, :]
    prefix = jnp.concatenate(pre, axis=0)
    sel = vis & (above | (tie & (prefix <= need)))

    outs = []
    group = ATT_HEADS // KV_HEADS
    for g in range(KV_HEADS):
        kg = k_ref[0][:, g * HEAD_DIM:(g + 1) * HEAD_DIM].astype(BF16)
        vtg = vt_ref[0][g * HEAD_DIM:(g + 1) * HEAD_DIM, :].astype(BF16)
        for hh in range(group):
            h = g * group + hh
            qh = q_ref[0][:, h * HEAD_DIM:(h + 1) * HEAD_DIM].astype(BF16)
            s = lax.dot_general(kg, qh, _NT, preferred_element_type=F32) * (HEAD_DIM ** -0.5)
            s = jnp.where(sel, s, -jnp.inf)
            m = jnp.max(s, axis=0, keepdims=True)
            p = jnp.exp(s - m)
            denom = jnp.sum(p, axis=0, keepdims=True)
            outs.append(jnp.dot(vtg, p.astype(BF16), preferred_element_type=F32) / denom)
    o_ref[0] = jnp.concatenate(outs, axis=0).T


def dsa_prompt_pallas(q2, k2, v2, qi2, ki, wi, topk):
    B, T = q2.shape[:2]
    tq = min(DSA_Q_TILE, T)
    chunk = min(DSA_PREFIX_CHUNK, T)
    groups = DSA_GROUPS if T % (DSA_GROUPS * chunk) == 0 else 1
    tg = T // groups
    assert tg % tq == 0 and tg % chunk == 0
    vt = jnp.swapaxes(v2, 1, 2)
    wit = jnp.swapaxes(wi, 1, 2)
    tri = jnp.tril(jnp.ones((chunk, chunk), BF16))
    outs = []
    for g in range(groups):
        s_len = (g + 1) * tg
        off = g * (tg // tq)
        kern = functools.partial(_dsa_prompt_kernel, topk=topk, q_start=g * tg)
        outs.append(pl.pallas_call(
            kern,
            grid=(B, tg // tq),
            in_specs=[pl.BlockSpec((1, tq, IDX_Q_W), lambda b, i, off=off: (b, off + i, 0)),
                      pl.BlockSpec((1, s_len, IDX_DIM), lambda b, i: (b, 0, 0)),
                      pl.BlockSpec((1, IDX_HEADS, tq), lambda b, i, off=off: (b, 0, off + i)),
                      pl.BlockSpec((1, tq, ATT_Q_W), lambda b, i, off=off: (b, off + i, 0)),
                      pl.BlockSpec((1, s_len, ATT_KV_W), lambda b, i: (b, 0, 0)),
                      pl.BlockSpec((1, ATT_KV_W, s_len), lambda b, i: (b, 0, 0)),
                      pl.BlockSpec((chunk, chunk), lambda b, i: (0, 0))],
            out_specs=pl.BlockSpec((1, tq, ATT_Q_W), lambda b, i: (b, i, 0)),
            out_shape=jax.ShapeDtypeStruct((B, tg, ATT_Q_W), F32),
            compiler_params=pltpu.CompilerParams(dimension_semantics=("arbitrary", "arbitrary"),
                                                 vmem_limit_bytes=VMEM_LIMIT),
            name="dsa_prompt_g%d" % g,
        )(qi2, ki, wit, q2, k2, vt, tri))
    return outs[0] if groups == 1 else jnp.concatenate(outs, axis=1)


LANES = 128
SMALL_W = LANES
SM_A = IDX_DIM
SM_B = IDX_DIM + DN_HEADS
SM_W = IDX_DIM + 2 * DN_HEADS
IN_GROUPS = (CONV_DIM, DN_V_W, ATT_Q_W, ATT_KV_W, ATT_KV_W, IDX_Q_W, SMALL_W, D_MODEL, D_MODEL)
IN_OFFS = tuple(int(o) for o in np.cumsum((0,) + IN_GROUPS))
N_IN_PAD = IN_OFFS[-1]
ROT = HEAD_DIM // ROT_FRACTION
ROT_HALF = ROT // 2
INPROJ_ROWS = 256


def regroup_w_in(w):
    (conv, z, a, b, q, k, v, qi, ki, wi, g_dn, g_att) = split_columns(w)
    pad = jnp.zeros((w.shape[0], SMALL_W - SM_W - IDX_HEADS), w.dtype)
    small = jnp.concatenate([ki, a, b, wi, pad], axis=1)
    return jnp.concatenate([conv, z, q, k, v, qi, small, g_dn, g_att], axis=1).astype(BF16)


def rope_tables(pos, small):
    inv_freq = ROPE_THETA ** (-jnp.arange(ROT_HALF, dtype=F32) / ROT_HALF)
    ang = pos.astype(F32)[:, None] * inv_freq[None, :]
    lane = jnp.arange(LANES)
    in_head = lane if small else lane % HEAD_DIM
    idx = lane % ROT_HALF
    cos = jnp.where(in_head[None, :] < ROT, jnp.cos(ang)[:, idx], 1.0)
    sin = jnp.sin(ang)[:, idx]
    sin_a = jnp.where(in_head[None, :] < ROT_HALF, -sin, 0.0)
    sin_b = jnp.where((in_head[None, :] >= ROT_HALF) & (in_head[None, :] < ROT), sin, 0.0)
    return jnp.stack([cos, sin_a, sin_b]).astype(F32)


def _rope_lanes(x, tab_ref):
    return (x * tab_ref[0] + pltpu.roll(x, LANES - ROT_HALF, 1) * tab_ref[1]
            + pltpu.roll(x, ROT_HALF, 1) * tab_ref[2])


def _inproj_kernel(x_ref, g_ref, w_ref, tab_ref, tabs_ref,
                   conv_ref, z_ref, q_ref, k_ref, v_ref, qi_ref, sm_ref, ga_ref, gb_ref):
    x = x_ref[...]
    ms = jnp.mean(x * x, axis=-1, keepdims=True)
    hn = (x * lax.rsqrt(ms + NORM_EPS) * g_ref[...]).astype(BF16)
    outs = (conv_ref, z_ref, q_ref, k_ref, v_ref, qi_ref, sm_ref, ga_ref, gb_ref)
    roped = (False, False, True, True, False, True, False, False, False)
    for gi, (o_ref, rp) in enumerate(zip(outs, roped)):
        y = jnp.dot(hn, w_ref[:, IN_OFFS[gi]:IN_OFFS[gi + 1]], preferred_element_type=F32)
        if o_ref is sm_ref:
            o_ref[...] = _rope_lanes(y, tabs_ref)
        elif rp:
            for c in range(IN_GROUPS[gi] // LANES):
                o_ref[:, c * LANES:(c + 1) * LANES] = _rope_lanes(y[:, c * LANES:(c + 1) * LANES], tab_ref)
        else:
            o_ref[...] = y


def inproj_pallas(x2d, g, w_grouped, tab, tab_small):
    n, d = x2d.shape
    tm = min(INPROJ_ROWS, n)
    p_len = tab.shape[1]
    assert n % tm == 0 and p_len % tm == 0
    tab_spec = pl.BlockSpec((3, tm, LANES), lambda i: (0, i % (p_len // tm), 0))
    return pl.pallas_call(
        _inproj_kernel,
        grid=(n // tm,),
        in_specs=[pl.BlockSpec((tm, d), lambda i: (i, 0)),
                  pl.BlockSpec((1, d), lambda i: (0, 0)),
                  pl.BlockSpec((d, N_IN_PAD), lambda i: (0, 0)),
                  tab_spec, tab_spec],
        out_specs=[pl.BlockSpec((tm, wd), lambda i: (i, 0)) for wd in IN_GROUPS],
        out_shape=[jax.ShapeDtypeStruct((n, wd), F32) for wd in IN_GROUPS],
        compiler_params=pltpu.CompilerParams(dimension_semantics=("arbitrary",), vmem_limit_bytes=VMEM_LIMIT),
        name="inproj",
    )(x2d, g.reshape(1, d), w_grouped, tab, tab_small)


DN_SUPER = 256
DN_SOLVE_PASSES = 3


def _split_bf16(x, pieces):
    parts = []
    r = x
    for _ in range(pieces):
        p = r.astype(BF16)
        parts.append(p)
        r = r - p.astype(F32)
    return parts


def _mm(a, b, passes=1):
    if passes == 1:
        return jnp.dot(a.astype(BF16), b.astype(BF16), preferred_element_type=F32)
    pa, pb = _split_bf16(a, 2), _split_bf16(b, 2)
    out = jnp.dot(pa[0], pb[0], preferred_element_type=F32)
    out = out + jnp.dot(pa[0], pb[1], preferred_element_type=F32)
    return out + jnp.dot(pa[1], pb[0], preferred_element_type=F32)


def _ones_mm(m01, x, left):
    out = None
    for p in _split_bf16(x, 3):
        t = jnp.dot(m01, p, preferred_element_type=F32) if left else jnp.dot(p, m01, preferred_element_type=F32)
        out = t if out is None else out + t
    return out


def _softplus(x):
    return jnp.maximum(x, 0.0) + jnp.log1p(jnp.exp(-jnp.abs(x)))


def _dn_prompt_kernel(cin_ref, z_ref, sm_ref, abt_ref, cw_ref, gp_ref, gpt_ref, gain_ref,
                      y_ref, sfin_ref, s_scr, ext_scr):
    i = pl.program_id(1)
    sc = cin_ref.shape[1]
    nchunk = sc // DN_CHUNK

    @pl.when(i == 0)
    def _():
        s_scr[...] = jnp.zeros_like(s_scr)
        ext_scr[0:8, :] = jnp.zeros((8, CONV_DIM), F32)

    ext_scr[8:8 + sc, :] = cin_ref[0]

    def conv_group(col):
        acc = jnp.zeros((sc, LANES), F32)
        for w in range(CONV_W):
            lo = 8 - (CONV_W - 1) + w
            acc = acc + ext_scr[lo:lo + sc, col:col + LANES] * cw_ref[w:w + 1, col:col + LANES]
        return acc * jax.nn.sigmoid(acc)

    ri = lax.broadcasted_iota(jnp.int32, (sc, sc), 0)
    ci = lax.broadcasted_iota(jnp.int32, (sc, sc), 1)
    shift = DN_CHUNK.bit_length() - 1
    same = (ri >> shift) == (ci >> shift)
    le = same & (ci <= ri)
    lt = same & (ci < ri)
    le_bf = le.astype(BF16)
    ge_bf = (same & (ci >= ri)).astype(BF16)
    same_bf = same.astype(BF16)
    eye = (ri == ci).astype(F32)

    sm = sm_ref[0]
    g_all = -jnp.exp(gp_ref[0:1, :]) * _softplus(sm + gp_ref[1:2, :])
    beta_all = jax.nn.sigmoid(sm)
    gc_all = _ones_mm(le_bf, g_all, True)
    gs_all = _ones_mm(same_bf, g_all, True)
    g_t = -jnp.exp(gpt_ref[:, 0:1]) * _softplus(abt_ref[0] + gpt_ref[:, 1:2])
    gc_t = _ones_mm(ge_bf, g_t, False)

    heads = []
    for h in range(DN_HEADS):
        q = conv_group(h * DN_DK)
        k = conv_group(DN_QK_W + h * DN_DK)
        v = conv_group(2 * DN_QK_W + h * DN_DV)
        q = q * lax.rsqrt(jnp.sum(q * q, axis=-1, keepdims=True) + NORM_EPS) * (DN_DK ** -0.5)
        k = k * lax.rsqrt(jnp.sum(k * k, axis=-1, keepdims=True) + NORM_EPS)
        gcol = gc_all[:, SM_A + h:SM_A + h + 1]
        grow = gc_t[h:h + 1, :]
        bcol = beta_all[:, SM_B + h:SM_B + h + 1]
        decay = jnp.where(le, jnp.exp(jnp.where(le, gcol - grow, 0.0)), 0.0)
        kb = k.astype(BF16)
        kk = lax.dot_general(kb, kb, _NT, preferred_element_type=F32)
        qk = lax.dot_general(q.astype(BF16), kb, _NT, preferred_element_type=F32)
        nmat = jnp.where(lt, bcol * kk * decay, 0.0)
        aqk = qk * decay
        xinv = eye - nmat
        pw = _mm(nmat, nmat, DN_SOLVE_PASSES)
        xinv = xinv + _mm(xinv, pw, DN_SOLVE_PASSES)
        for _ in range(4):
            pw = _mm(pw, pw, DN_SOLVE_PASSES)
            xinv = xinv + _mm(xinv, pw, DN_SOLVE_PASSES)
        egc = jnp.exp(gcol)
        rhs = jnp.concatenate([v * bcol, k * (bcol * egc)], axis=1)
        sol = _mm(xinv, rhs, DN_SOLVE_PASSES)
        gtot = gs_all[:, SM_A + h:SM_A + h + 1]
        heads.append(dict(val=sol[:, :DN_DV], kcd=sol[:, DN_DV:], aqk=aqk, qdec=q * egc,
                          kdec_t=(k * jnp.exp(gtot - gcol)).T, gtot=gtot))

    outs = [[] for _ in range(DN_HEADS)]
    for c in range(nchunk):
        r0 = c * DN_CHUNK
        for h, hd in enumerate(heads):
            s_old = s_scr[h]
            v_new = hd["val"][r0:r0 + DN_CHUNK] - _mm(hd["kcd"][r0:r0 + DN_CHUNK], s_old)
            o = _mm(hd["qdec"][r0:r0 + DN_CHUNK], s_old) + _mm(hd["aqk"][r0:r0 + DN_CHUNK, r0:r0 + DN_CHUNK], v_new)
            s_scr[h] = s_old * jnp.exp(hd["gtot"][r0:r0 + 1]) + _mm(hd["kdec_t"][:, r0:r0 + DN_CHUNK], v_new)
            outs[h].append(o)

    for h in range(DN_HEADS):
        o = jnp.concatenate(outs[h], axis=0)
        on = o * lax.rsqrt(jnp.mean(o * o, axis=-1, keepdims=True) + NORM_EPS) * gain_ref[...]
        zz = z_ref[0, :, h * DN_DV:(h + 1) * DN_DV]
        y_ref[0, :, h * DN_DV:(h + 1) * DN_DV] = on * (zz * jax.nn.sigmoid(zz))

    ext_scr[0:8, :] = ext_scr[sc:sc + 8, :]

    @pl.when(i == pl.num_programs(1) - 1)
    def _():
        sfin_ref[0] = s_scr[...]


def deltanet_prompt_pallas(conv_in, z, small, conv_w, a_log, dt_bias, gain):
    B, T, _ = conv_in.shape
    sc = min(DN_SUPER, T)
    assert T % sc == 0 and sc % DN_CHUNK == 0
    abt = jnp.swapaxes(small[:, :, SM_A:SM_A + 2 * DN_HEADS], 1, 2)
    lane_par = jnp.zeros((2, LANES), F32).at[0, SM_A:SM_A + DN_HEADS].set(a_log).at[1, SM_A:SM_A + DN_HEADS].set(dt_bias)
    row_par = jnp.zeros((2 * DN_HEADS, 2), F32).at[:DN_HEADS, 0].set(a_log).at[:DN_HEADS, 1].set(dt_bias)
    return pl.pallas_call(
        _dn_prompt_kernel,
        grid=(B, T // sc),
        in_specs=[pl.BlockSpec((1, sc, CONV_DIM), lambda b, i: (b, i, 0)),
                  pl.BlockSpec((1, sc, DN_V_W), lambda b, i: (b, i, 0)),
                  pl.BlockSpec((1, sc, LANES), lambda b, i: (b, i, 0)),
                  pl.BlockSpec((1, 2 * DN_HEADS, sc), lambda b, i: (b, 0, i)),
                  pl.BlockSpec((CONV_W, CONV_DIM), lambda b, i: (0, 0)),
                  pl.BlockSpec((2, LANES), lambda b, i: (0, 0)),
                  pl.BlockSpec((2 * DN_HEADS, 2), lambda b, i: (0, 0)),
                  pl.BlockSpec((1, DN_DV), lambda b, i: (0, 0))],
        out_specs=[pl.BlockSpec((1, sc, DN_V_W), lambda b, i: (b, i, 0)),
                   pl.BlockSpec((1, DN_HEADS, DN_DK, DN_DV), lambda b, i: (b, 0, 0, 0))],
        out_shape=[jax.ShapeDtypeStruct((B, T, DN_V_W), F32),
                   jax.ShapeDtypeStruct((B, DN_HEADS, DN_DK, DN_DV), F32)],
        scratch_shapes=[pltpu.VMEM((DN_HEADS, DN_DK, DN_DV), F32), pltpu.VMEM((sc + 8, CONV_DIM), F32)],
        compiler_params=pltpu.CompilerParams(dimension_semantics=("arbitrary", "arbitrary"),
                                             vmem_limit_bytes=VMEM_LIMIT),
        name="deltanet_prompt",
    )(conv_in, z, small, abt, conv_w, lane_par, row_par, gain.reshape(1, DN_DV))


def rms_norm(x, g):
    xf = x.astype(F32)
    y = xf * lax.rsqrt(jnp.mean(xf * xf, axis=-1, keepdims=True) + NORM_EPS)
    return (y * g.astype(F32)).astype(x.dtype)


def l2_normalize(x):
    return x * lax.rsqrt(jnp.sum(x * x, axis=-1, keepdims=True) + NORM_EPS)


def rope_partial(x, pos):
    rot = x.shape[-1] // ROT_FRACTION
    half = rot // 2
    inv_freq = ROPE_THETA ** (-jnp.arange(half, dtype=F32) / half)
    ang = pos.astype(F32)[:, None] * inv_freq[None, :]
    cos = jnp.cos(ang)[:, None, :]
    sin = jnp.sin(ang)[:, None, :]
    xr = x[..., :rot].astype(F32)
    x1, x2 = xr[..., :half], xr[..., half:]
    rotated = jnp.concatenate([x1 * cos - x2 * sin, x2 * cos + x1 * sin], axis=-1)
    return jnp.concatenate([rotated.astype(x.dtype), x[..., rot:]], axis=-1)


def split_columns(z):
    offsets = np.cumsum(IN_SPLITS)[:-1].tolist()
    return jnp.split(z, offsets, axis=-1)


def causal_dwconv_silu(xp, w):
    y = lax.conv_general_dilated(xp, w[:, None, :].astype(xp.dtype), window_strides=(1,), padding='VALID',
                                 dimension_numbers=('NWC', 'WIO', 'NWC'), feature_group_count=xp.shape[-1])
    return jax.nn.silu(y)


def dn_features(c, a, b, a_log, dt_bias):
    B, T, _ = c.shape
    cf = c.astype(F32)
    q = l2_normalize(cf[..., :DN_QK_W].reshape(B, T, DN_HEADS, DN_DK)) * (DN_DK ** -0.5)
    k = l2_normalize(cf[..., DN_QK_W:2 * DN_QK_W].reshape(B, T, DN_HEADS, DN_DK))
    v = cf[..., 2 * DN_QK_W:].reshape(B, T, DN_HEADS, DN_DV)
    beta = jax.nn.sigmoid(b.astype(F32))
    g = -jnp.exp(a_log.astype(F32)) * jax.nn.softplus(a.astype(F32) + dt_bias.astype(F32))
    return q, k, v, g, beta


def dn_output(o, z, gain, dtype):
    B, T = o.shape[:2]
    on = o * lax.rsqrt(jnp.mean(o * o, axis=-1, keepdims=True) + NORM_EPS) * gain.astype(F32)
    y = on * jax.nn.silu(z.astype(F32)).reshape(B, T, DN_HEADS, DN_DV)
    return y.reshape(B, T, DN_V_W).astype(dtype)


def gated_delta_chunked(q, k, v, g, beta):
    B, T, H, DK = q.shape
    DV = v.shape[-1]
    C = DN_CHUNK
    NC = T // C

    def chunks(x):
        return jnp.moveaxis(x.reshape((B, NC, C, H) + x.shape[3:]), 3, 1)

    qc, kc, vc, bc = chunks(q), chunks(k), chunks(v), chunks(beta)
    gc = jnp.cumsum(chunks(g), axis=-1)
    tri = jnp.tril(jnp.ones((C, C), bool))
    strict = jnp.tril(jnp.ones((C, C), bool), -1)
    diff = gc[..., :, None] - gc[..., None, :]
    decay = jnp.where(tri, jnp.exp(jnp.where(tri, diff, 0.0)), 0.0)
    kk = jnp.einsum('bhnid,bhnjd->bhnij', kc, kc)
    a_mat = jnp.where(strict, bc[..., :, None] * kk * decay, 0.0) + jnp.eye(C, dtype=F32)
    rhs = jnp.concatenate([vc * bc[..., None], kc * (bc * jnp.exp(gc))[..., None]], axis=-1)
    sol = lax.linalg.triangular_solve(a_mat, rhs, left_side=True, lower=True)
    val, kcd = sol[..., :DV], sol[..., DV:]
    a_qk = jnp.einsum('bhnid,bhnjd->bhnij', qc, kc) * decay
    q_dec = qc * jnp.exp(gc)[..., None]
    g_last = gc[..., -1]
    k_dec = kc * jnp.exp(g_last[..., None] - gc)[..., None]

    def step(S, xs):
        qd, kd, vl, kcd_c, aqk, gl = xs
        v_new = vl - jnp.einsum('bhck,bhkv->bhcv', kcd_c, S)
        o = jnp.einsum('bhck,bhkv->bhcv', qd, S) + jnp.einsum('bhcj,bhjv->bhcv', aqk, v_new)
        S = S * jnp.exp(gl)[..., None, None] + jnp.einsum('bhck,bhcv->bhkv', kd, v_new)
        return S, o

    xs = tuple(jnp.moveaxis(t, 2, 0) for t in (q_dec, k_dec, val, kcd, a_qk, g_last))
    S_fin, o = lax.scan(step, jnp.zeros((B, H, DK, DV), F32), xs)
    o = jnp.transpose(o, (1, 0, 3, 2, 4)).reshape(B, T, H, DV)
    return o, S_fin


def gated_delta_recurrent(q, k, v, g, beta, S0):
    def step(S, xs):
        qt, kt, vt, gt, bt = xs
        S = S * jnp.exp(gt)[..., None, None]
        pred = jnp.einsum('bhk,bhkv->bhv', kt, S)
        S = S + jnp.einsum('bhk,bhv->bhkv', kt, bt[..., None] * (vt - pred))
        return S, jnp.einsum('bhk,bhkv->bhv', qt, S)

    xs = tuple(jnp.moveaxis(t, 1, 0) for t in (q, k, v, g, beta))
    S_fin, o = lax.scan(step, S0, xs)
    return jnp.moveaxis(o, 0, 1), S_fin


def dsa_features(q, k, v, qi, ki, pos):
    B, T, _ = q.shape
    q = rope_partial(q.reshape(B, T, ATT_HEADS, HEAD_DIM), pos)
    k = rope_partial(k.reshape(B, T, KV_HEADS, HEAD_DIM), pos)
    v = v.reshape(B, T, KV_HEADS, HEAD_DIM)
    qi = rope_partial(qi.reshape(B, T, IDX_HEADS, IDX_DIM), pos)
    ki = rope_partial(ki.reshape(B, T, 1, IDX_DIM), pos)[:, :, 0]
    return q, k, v, qi, ki


def index_scores(qi, ki, wi):
    s = jax.nn.relu(jnp.einsum('bqhd,bsd->bqhs', qi, ki).astype(F32) * (IDX_DIM ** -0.5))
    return jnp.einsum('bqh,bqhs->bqs', wi.astype(F32) * (IDX_HEADS ** -0.5), s)


def sparse_attend(q, ks, vs, valid):
    B, Q, H, HD = q.shape
    qg = q.reshape(B, Q, KV_HEADS, H // KV_HEADS, HD)
    s = jnp.einsum('bqngd,bqknd->bqngk', qg, ks).astype(F32) * (HD ** -0.5)
    s = jnp.where(valid[:, :, None, None, :], s, -jnp.inf)
    p = jax.nn.softmax(s, axis=-1).astype(vs.dtype)
    o = jnp.einsum('bqngk,bqknd->bqngd', p, vs)
    return o.reshape(B, Q, H * HD)


def dsa_prompt(q, k, v, qi, ki, wi, topk):
    B, T = q.shape[:2]
    nb = T // Q_BLOCK
    bidx = jnp.arange(B)[:, None, None]
    kpos = jnp.arange(T)

    def blocks(x):
        return jnp.swapaxes(x.reshape((B, nb, Q_BLOCK) + x.shape[2:]), 0, 1)

    def one_block(args):
        qb, qib, wib, t0 = args
        tpos = t0 + jnp.arange(Q_BLOCK)
        visible = kpos[None, :] <= tpos[:, None]
        score = jnp.where(visible[None], index_scores(qib, ki, wib), -jnp.inf)
        _, sel = lax.top_k(score, topk)
        valid = sel <= tpos[None, :, None]
        return sparse_attend(qb, k[bidx, sel], v[bidx, sel], valid)

    out = lax.map(one_block, (blocks(q), blocks(qi), blocks(wi), jnp.arange(nb) * Q_BLOCK))
    return jnp.swapaxes(out, 0, 1).reshape(B, T, ATT_Q_W)


def dsa_sample(q, k_new, v_new, qi, ki_new, wi, cache_k, cache_v, cache_kidx, page_table, layer, topk):
    B, S = q.shape[:2]
    past = page_table.shape[1] * PAGE_SIZE
    slots = jnp.arange(PAGE_SIZE)
    ki_past = cache_kidx[page_table[:, :, None], slots[None, None, :], layer].reshape(B, past, IDX_DIM)
    ki_all = jnp.concatenate([ki_past.astype(ki_new.dtype), ki_new], axis=1)
    qpos = past + jnp.arange(S)
    kpos = jnp.arange(past + S)
    visible = kpos[None, :] <= qpos[:, None]
    score = jnp.where(visible[None], index_scores(qi, ki_all, wi), -jnp.inf)
    _, sel = lax.top_k(score, topk)
    in_past = sel < past
    ps = jnp.minimum(sel, past - 1)
    phys = jnp.take_along_axis(page_table, (ps // PAGE_SIZE).reshape(B, S * topk), axis=1).reshape(B, S, topk)
    slot = ps % PAGE_SIZE
    bidx = jnp.arange(B)[:, None, None]
    ns = jnp.clip(sel - past, 0, S - 1)
    sel_k = jnp.where(in_past[..., None, None], cache_k[phys, slot, layer].astype(k_new.dtype), k_new[bidx, ns])
    sel_v = jnp.where(in_past[..., None, None], cache_v[phys, slot, layer].astype(v_new.dtype), v_new[bidx, ns])
    valid = sel <= qpos[None, :, None]
    return sparse_attend(q, sel_k, sel_v, valid)


def branch_merge(o_dn, o_att, g_dn, g_att, w_dn_up, w_att_up, w_o):
    y = jax.nn.sigmoid(g_dn) * (o_dn @ w_dn_up) + jax.nn.sigmoid(g_att) * (o_att @ w_att_up)
    return y @ w_o


def mem_kv(mem, g, w_mk, w_mv):
    B, M, _ = mem.shape
    m = rms_norm(mem, g)
    return ((m @ w_mk).reshape(B, M, MEM_HEADS, MEM_HEAD_DIM), (m @ w_mv).reshape(B, M, MEM_HEADS, MEM_HEAD_DIM))


def mem_attend(x, mk, mv, g, w_mq, w_mo):
    B, T, _ = x.shape
    q = (rms_norm(x, g) @ w_mq).reshape(B, T, MEM_HEADS, MEM_HEAD_DIM)
    s = jnp.einsum('bthd,bmhd->bhtm', q, mk.astype(q.dtype)).astype(F32) * (MEM_HEAD_DIM ** -0.5)
    p = jax.nn.softmax(s, axis=-1).astype(q.dtype)
    o = jnp.einsum('bhtm,bmhd->bthd', p, mv.astype(q.dtype)).reshape(B, T, MEM_W)
    return o @ w_mo


def peer_block(h, wq, sub_keys, u, v):
    n = h.shape[0]
    q = (h @ wq).reshape(n, PEER_HEADS, 2, PEER_QDIM // 2)
    s = jnp.einsum('nhcd,hckd->nhck', q, sub_keys).astype(F32)
    s_top, i_top = lax.top_k(s, PEER_TOPK)
    cand = (s_top[:, :, 0, :, None] + s_top[:, :, 1, None, :]).reshape(n, PEER_HEADS, PEER_TOPK * PEER_TOPK)
    cidx = (i_top[:, :, 0, :, None] * N_KEYS + i_top[:, :, 1, None, :]).reshape(n, PEER_HEADS, PEER_TOPK * PEER_TOPK)
    best, pos = lax.top_k(cand, PEER_TOPK)
    eidx = jnp.take_along_axis(cidx, pos, axis=-1).reshape(n, PEER_HEADS * PEER_TOPK)
    gate = jax.nn.softmax(best, axis=-1).reshape(n, PEER_HEADS * PEER_TOPK)
    act = jax.nn.gelu(jnp.einsum('nd,ned->ne', h, u[eidx]).astype(F32), approximate=False)
    return jnp.einsum('ne,ned->nd', (gate * act).astype(h.dtype), v[eidx])


def peer_ffn(h, wq, sub_keys, u, v):
    B, T, D = h.shape
    n = B * T
    flat = h.reshape(n, D)
    if n % PEER_BLOCK == 0 and n > PEER_BLOCK:
        out = lax.map(lambda hb: peer_block(hb, wq, sub_keys, u, v), flat.reshape(n // PEER_BLOCK, PEER_BLOCK, D))
        out = out.reshape(n, D)
    else:
        out = peer_block(flat, wq, sub_keys, u, v)
    return out.reshape(B, T, D)


def kernel(x_prompt, x_sample, cache_k, cache_v, cache_kidx, state_conv, state_delta, cache_mem_k, cache_mem_v,
           page_table, mem_prompt, norm_mix, w_in, dn_conv_w, dn_a_log, dn_dt_bias, dn_norm, w_dn_up, w_att_up,
           w_o, norm_mem_q, norm_mem_kv, w_mq, w_mk, w_mv, w_mo, norm_ffn, peer_wq, peer_keys, peer_u, peer_v,
           norm_final):
    B, T, _ = x_prompt.shape
    S = x_sample.shape[1]
    past = page_table.shape[1] * PAGE_SIZE
    pos_p = jnp.arange(T)
    pos_s = past + jnp.arange(S)
    topk_p = min(TOPK_MAX, T // 4)
    topk_s = min(TOPK_MAX, (past + S) // 4)

    hp, hs = x_prompt, x_sample
    p_k, p_v, p_kidx, p_conv, p_delta, p_mk, p_mv = [], [], [], [], [], [], []
    s_k, s_v, s_kidx, s_conv, s_delta = [], [], [], [], []
    Bs = x_sample.shape[0]
    tab_p, tab_p_small = rope_tables(pos_p, False), rope_tables(pos_p, True)
    pos_rows = jnp.tile(pos_s, Bs)
    tab_s, tab_s_small = rope_tables(pos_rows, False), rope_tables(pos_rows, True)
    for l in range(DEPTH):
        w_grouped = regroup_w_in(w_in[l])
        proj = inproj_pallas(hp.reshape(B * T, D_MODEL), norm_mix[l], w_grouped, tab_p, tab_p_small)
        conv_in, z, q, k, v, qi, small, g_dn, g_att = [t.reshape(B, T, -1) for t in proj]
        o_dn, st = deltanet_prompt_pallas(conv_in, z, small, dn_conv_w[l], dn_a_log[l], dn_dt_bias[l], dn_norm[l])
        aki = small[..., :IDX_DIM]
        wi = small[..., SM_W:SM_W + IDX_HEADS]
        ak = k.reshape(B, T, KV_HEADS, HEAD_DIM)
        av = v.reshape(B, T, KV_HEADS, HEAD_DIM)
        o_att = dsa_prompt_pallas(q, k, v, qi, aki, wi, topk_p)
        cbuf = conv_in
        hp = hp + branch_merge(o_dn, o_att, g_dn, g_att, w_dn_up[l], w_att_up[l], w_o[l])
        mk, mv = mem_kv(mem_prompt, norm_mem_kv[l], w_mk[l], w_mv[l])
        hp = hp + mem_attend(hp, mk, mv, norm_mem_q[l], w_mq[l], w_mo[l])
        peer_w = peer_prepare(peer_wq[l], peer_keys[l], peer_u[l], peer_v[l])
        hp = peer_residual_pallas(hp.reshape(B * T, D_MODEL), norm_ffn[l], peer_w).reshape(B, T, D_MODEL)
        p_k.append(ak)
        p_v.append(av)
        p_kidx.append(aki)
        p_conv.append(cbuf[:, -(CONV_W - 1):])
        p_delta.append(st)
        p_mk.append(mk)
        p_mv.append(mv)

        proj = inproj_pallas(hs.reshape(Bs * S, D_MODEL), norm_mix[l], w_grouped, tab_s, tab_s_small)
        conv_in, z, q, k, v, qi, small, g_dn, g_att = [t.reshape(Bs, S, -1) for t in proj]
        a = small[..., SM_A:SM_A + DN_HEADS]
        b = small[..., SM_B:SM_B + DN_HEADS]
        aki = small[..., :IDX_DIM]
        wi = small[..., SM_W:SM_W + IDX_HEADS]
        cbuf = jnp.concatenate([state_conv[l].astype(conv_in.dtype), conv_in], axis=1)
        dq, dk, dv, dg, dbeta = dn_features(causal_dwconv_silu(cbuf, dn_conv_w[l]), a, b, dn_a_log[l], dn_dt_bias[l])
        o, st = gated_delta_recurrent(dq, dk, dv, dg, dbeta, state_delta[l].astype(F32))
        o_dn = dn_output(o, z, dn_norm[l], hs.dtype)
        aq = q.reshape(Bs, S, ATT_HEADS, HEAD_DIM)
        ak = k.reshape(Bs, S, KV_HEADS, HEAD_DIM)
        av = v.reshape(Bs, S, KV_HEADS, HEAD_DIM)
        aqi = qi.reshape(Bs, S, IDX_HEADS, IDX_DIM)
        o_att = dsa_sample(aq, ak, av, aqi, aki, wi, cache_k, cache_v, cache_kidx, page_table, l, topk_s)
        hs = hs + branch_merge(o_dn, o_att, g_dn, g_att, w_dn_up[l], w_att_up[l], w_o[l])
        hs = hs + mem_attend(hs, cache_mem_k[l], cache_mem_v[l], norm_mem_q[l], w_mq[l], w_mo[l])
        hs = peer_residual_pallas(hs.reshape(Bs * S, D_MODEL), norm_ffn[l], peer_w).reshape(Bs, S, D_MODEL)
        s_k.append(ak)
        s_v.append(av)
        s_kidx.append(aki)
        s_conv.append(cbuf[:, -(CONV_W - 1):])
        s_delta.append(st)

    y_prompt = pallas_rms_norm(hp.reshape(B * T, D_MODEL), norm_final, 512).reshape(B, T, D_MODEL)
    y_sample = pallas_rms_norm(hs.reshape(-1, D_MODEL), norm_final, 32).reshape(hs.shape)
    return (y_prompt, y_sample,
            jnp.stack(p_k, axis=2), jnp.stack(p_v, axis=2), jnp.stack(p_kidx, axis=2),
            jnp.stack(p_conv, axis=0), jnp.stack(p_delta, axis=0),
            jnp.stack(p_mk, axis=0), jnp.stack(p_mv, axis=0),
            jnp.stack(s_k, axis=2), jnp.stack(s_v, axis=2), jnp.stack(s_kidx, axis=2),
            jnp.stack(s_conv, axis=0), jnp.stack(s_delta, axis=0))
```

```python
import functools

import jax
import jax.numpy as jnp
import numpy as np
from jax import lax
from jax.experimental import pallas as pl
from jax.experimental.pallas import tpu as pltpu

D_MODEL = 1024
DEPTH = 2
PAGE_SIZE = 128
DN_HEADS = 4
DN_DK = 128
DN_DV = 128
CONV_W = 4
DN_CHUNK = 64
ATT_HEADS = 8
KV_HEADS = 2
HEAD_DIM = 64
IDX_HEADS = 4
IDX_DIM = 64
TOPK_MAX = 256
ROPE_THETA = 500000.0
ROT_FRACTION = 4
Q_BLOCK = 128
MEM_LEN = 256
MEM_HEADS = 4
MEM_HEAD_DIM = 128
N_KEYS = 128
N_EXPERTS = N_KEYS * N_KEYS
PEER_HEADS = 8
PEER_QDIM = 256
PEER_TOPK = 16
PEER_BLOCK = 128
NORM_EPS = 1e-6

DN_QK_W = DN_HEADS * DN_DK
DN_V_W = DN_HEADS * DN_DV
CONV_DIM = 2 * DN_QK_W + DN_V_W
ATT_Q_W = ATT_HEADS * HEAD_DIM
ATT_KV_W = KV_HEADS * HEAD_DIM
IDX_Q_W = IDX_HEADS * IDX_DIM
MEM_W = MEM_HEADS * MEM_HEAD_DIM
IN_SPLITS = (CONV_DIM, DN_V_W, DN_HEADS, DN_HEADS, ATT_Q_W, ATT_KV_W, ATT_KV_W, IDX_Q_W, IDX_DIM, IDX_HEADS,
             D_MODEL, D_MODEL)
F32 = jnp.float32


def _rms_kernel(x_ref, g_ref, o_ref):
    x = x_ref[...]
    ms = jnp.mean(x * x, axis=-1, keepdims=True)
    o_ref[...] = x * lax.rsqrt(ms + NORM_EPS) * g_ref[...]


def pallas_rms_norm(x2d, g, tm):
    n, d = x2d.shape
    return pl.pallas_call(
        _rms_kernel,
        grid=(n // tm,),
        in_specs=[pl.BlockSpec((tm, d), lambda i: (i, 0)), pl.BlockSpec((1, d), lambda i: (0, 0))],
        out_specs=pl.BlockSpec((tm, d), lambda i: (i, 0)),
        out_shape=jax.ShapeDtypeStruct((n, d), F32),
        name="rms_norm",
    )(x2d, g.reshape(1, d))


BF16 = jnp.bfloat16
_NT = (((1,), (1,)), ((), ()))
PEER_GATE_TOKENS = 256
PEER_DENSE_TOKENS = 512
PEER_EXPERT_BLOCK = 1024
VMEM_LIMIT = 48 * 1024 * 1024
_STAIR = [(a, b) for a in range(PEER_TOPK) for b in range(PEER_TOPK) if (a + 1) * (b + 1) <= PEER_TOPK]


def _peer_gate_kernel(x_ref, g_ref, wqt_ref, keys_ref, hn_ref, e1_ref, lrow_ref, e2_ref, r2_ref,
                      s_scr, rank_scr, tv_scr):
    x = x_ref[...]
    ms = jnp.mean(x * x, axis=-1, keepdims=True)
    hn = (x * lax.rsqrt(ms + NORM_EPS) * g_ref[...]).astype(BF16)
    hn_ref[...] = hn
    qt = lax.dot_general(wqt_ref[...], hn, _NT, preferred_element_type=F32)
    half = PEER_QDIM // 2
    for p in range(2 * PEER_HEADS):
        qp = qt[p * half:(p + 1) * half, :].astype(BF16)
        s_scr[p] = jnp.dot(keys_ref[p], qp, preferred_element_type=F32)

    def extract(p, carry):
        cur = s_scr[p]
        rank = jnp.full(cur.shape, float(PEER_TOPK), F32)
        tops = []
        for r in range(PEER_TOPK):
            m = jnp.max(cur, axis=0, keepdims=True)
            hit = cur == m
            rank = jnp.where(hit, float(r), rank)
            cur = jnp.where(hit, -jnp.inf, cur)
            tops.append(m)
        rank_scr[p] = rank
        tv_scr[p] = jnp.concatenate(tops, axis=0)
        return carry

    lax.fori_loop(0, 2 * PEER_HEADS, extract, 0)

    def per_head(h, carry):
        tv1 = tv_scr[2 * h]
        tv2 = tv_scr[2 * h + 1]
        m1 = tv1[0:1]
        m2 = tv2[0:1]
        rows = [tv1[a:a + 1] + tv2[b:b + 1] for (a, b) in _STAIR]
        pad = (-len(rows)) % 8
        rows += [jnp.full_like(m1, -jnp.inf)] * pad
        cand = jnp.concatenate(rows, axis=0)
        n_gt = jnp.zeros(cand.shape, F32)
        for r in rows[:len(_STAIR)]:
            n_gt = n_gt + (r > cand).astype(F32)
        thr = jnp.min(jnp.where(n_gt <= float(PEER_TOPK - 1), cand, jnp.inf), axis=0, keepdims=True)
        ex1 = jnp.exp(tv1 - m1)
        ex2 = jnp.exp(tv2 - m2)
        l_rows = []
        z = jnp.zeros_like(m1)
        for a in range(PEER_TOPK):
            sel = (tv1[a:a + 1] + tv2) >= thr
            l_rows.append(jnp.sum(sel.astype(F32), axis=0, keepdims=True))
            z = z + ex1[a:a + 1] * jnp.sum(jnp.where(sel, ex2, 0.0), axis=0, keepdims=True)
        rank1 = rank_scr[2 * h]
        rank2 = rank_scr[2 * h + 1]
        lrow = jnp.zeros(rank1.shape, F32)
        for a in range(PEER_TOPK):
            lrow = jnp.where(rank1 == float(a), l_rows[a], lrow)
        in1 = rank1 < float(PEER_TOPK)
        in2 = rank2 < float(PEER_TOPK)
        e1_ref[h] = jnp.where(in1, jnp.exp(jnp.where(in1, s_scr[2 * h] - m1, 0.0)), 0.0) / z
        lrow_ref[h] = lrow
        e2_ref[h] = jnp.where(in2, jnp.exp(jnp.where(in2, s_scr[2 * h + 1] - m2, 0.0)), 0.0).astype(BF16)
        r2_ref[h] = rank2.astype(BF16)
        return carry

    lax.fori_loop(0, PEER_HEADS, per_head, 0)


def _peer_dense_kernel(hn_ref, u_ref, vt_ref, e1_ref, lrow_ref, e2_ref, r2_ref, hp_ref, o_ref, acc_ref, wt_ref):
    j = pl.program_id(1)

    @pl.when(j == 0)
    def _():
        acc_ref[...] = jnp.zeros_like(acc_ref)

    act = lax.dot_general(u_ref[...], hn_ref[...], _NT, preferred_element_type=F32)
    rows_per_step = PEER_EXPERT_BLOCK // N_KEYS
    for ii in range(rows_per_step):
        a = act[ii * N_KEYS:(ii + 1) * N_KEYS, :]
        ga = 0.5 * a * (1.0 + lax.erf(a * (2.0 ** -0.5)))
        gate = jnp.zeros(a.shape, BF16)
        zero = jnp.zeros(a.shape, BF16)
        for h in range(PEER_HEADS):
            lr = lrow_ref[h, ii:ii + 1, :].astype(BF16)
            e1 = e1_ref[h, ii:ii + 1, :].astype(BF16)
            gate = gate + jnp.where(r2_ref[h] < lr, e2_ref[h], zero) * e1
        wt_ref[ii * N_KEYS:(ii + 1) * N_KEYS, :] = gate * ga.astype(BF16)
    acc_ref[...] += jnp.dot(vt_ref[...], wt_ref[...], preferred_element_type=F32)

    @pl.when(j == pl.num_programs(1) - 1)
    def _():
        o_ref[...] = hp_ref[...] + acc_ref[...].T


def peer_prepare(wq, sub_keys, u, v):
    wqt = wq.T.astype(BF16)
    keys = sub_keys.reshape(2 * PEER_HEADS, N_KEYS, PEER_QDIM // 2).astype(BF16)
    return wqt, keys, u.astype(BF16), v.T.astype(BF16)


def peer_residual_pallas(hp2d, g, prepared):
    n, d = hp2d.shape
    tg, td = min(PEER_GATE_TOKENS, n), min(PEER_DENSE_TOKENS, n)
    nq = PEER_HEADS * PEER_QDIM
    wqt, keys, u_bf, vt_bf = prepared
    gate_shape = jax.ShapeDtypeStruct((PEER_HEADS, N_KEYS, n), F32)
    gate_shape16 = jax.ShapeDtypeStruct((PEER_HEADS, N_KEYS, n), BF16)
    gate_spec = pl.BlockSpec((PEER_HEADS, N_KEYS, tg), lambda i: (0, 0, i))
    hn, e1, lrow, e2, r2 = pl.pallas_call(
        _peer_gate_kernel,
        grid=(n // tg,),
        in_specs=[pl.BlockSpec((tg, d), lambda i: (i, 0)),
                  pl.BlockSpec((1, d), lambda i: (0, 0)),
                  pl.BlockSpec((nq, d), lambda i: (0, 0)),
                  pl.BlockSpec((2 * PEER_HEADS, N_KEYS, PEER_QDIM // 2), lambda i: (0, 0, 0))],
        out_specs=[pl.BlockSpec((tg, d), lambda i: (i, 0)), gate_spec, gate_spec, gate_spec, gate_spec],
        out_shape=[jax.ShapeDtypeStruct((n, d), BF16), gate_shape, gate_shape, gate_shape16, gate_shape16],
        scratch_shapes=[pltpu.VMEM((2 * PEER_HEADS, N_KEYS, tg), F32),
                        pltpu.VMEM((2 * PEER_HEADS, N_KEYS, tg), F32),
                        pltpu.VMEM((2 * PEER_HEADS, PEER_TOPK, tg), F32)],
        compiler_params=pltpu.CompilerParams(dimension_semantics=("arbitrary",), vmem_limit_bytes=VMEM_LIMIT),
        name="peer_gate",
    )(hp2d, g.reshape(1, d), wqt, keys)

    rows = PEER_EXPERT_BLOCK // N_KEYS
    full_spec = pl.BlockSpec((PEER_HEADS, N_KEYS, td), lambda i, j: (0, 0, i))
    row_spec = pl.BlockSpec((PEER_HEADS, rows, td), lambda i, j: (0, j, i))
    return pl.pallas_call(
        _peer_dense_kernel,
        grid=(n // td, N_EXPERTS // PEER_EXPERT_BLOCK),
        in_specs=[pl.BlockSpec((td, d), lambda i, j: (i, 0)),
                  pl.BlockSpec((PEER_EXPERT_BLOCK, d), lambda i, j: (j, 0)),
                  pl.BlockSpec((d, PEER_EXPERT_BLOCK), lambda i, j: (0, j)),
                  row_spec, row_spec, full_spec, full_spec,
                  pl.BlockSpec((td, d), lambda i, j: (i, 0))],
        out_specs=pl.BlockSpec((td, d), lambda i, j: (i, 0)),
        out_shape=jax.ShapeDtypeStruct((n, d), F32),
        scratch_shapes=[pltpu.VMEM((d, td), F32), pltpu.VMEM((PEER_EXPERT_BLOCK, td), BF16)],
        compiler_params=pltpu.CompilerParams(dimension_semantics=("arbitrary", "arbitrary"),
                                             vmem_limit_bytes=VMEM_LIMIT),
        name="peer_dense",
    )(hn, u_bf, vt_bf, e1, lrow, e2, r2, hp2d)


DSA_Q_TILE = 128
DSA_PREFIX_CHUNK = 512
DSA_GROUPS = 4


def _dsa_prompt_kernel(qi_ref, ki_ref, wit_ref, q_ref, k_ref, vt_ref, tri_ref, o_ref, *, topk, q_start):
    tq = q_ref.shape[1]
    s_len = k_ref.shape[1]
    t0 = q_start + pl.program_id(1) * tq
    ki = ki_ref[0].astype(BF16)
    qi = qi_ref[0].astype(BF16)
    score = jnp.zeros((s_len, tq), F32)
    for h in range(IDX_HEADS):
        sh = lax.dot_general(ki, qi[:, h * IDX_DIM:(h + 1) * IDX_DIM], _NT, preferred_element_type=F32)
        score = score + jnp.maximum(sh * (IDX_DIM ** -0.5), 0.0) * (wit_ref[0, h:h + 1, :] * (IDX_HEADS ** -0.5))
    kpos = lax.broadcasted_iota(jnp.int32, (s_len, tq), 0)
    tpos = t0 + lax.broadcasted_iota(jnp.int32, (s_len, tq), 1)
    vis = kpos <= tpos
    score = jnp.where(vis, score + 0.0, -jnp.inf)
    bits = pltpu.bitcast(score, jnp.int32)
    key = bits ^ ((bits >> 31) & jnp.int32(0x7FFFFFFF))

    def search(i, best):
        cand = best + jnp.left_shift(jnp.int32(1), 31 - i)
        cnt = jnp.sum((key >= cand).astype(jnp.int32), axis=0, keepdims=True)
        return jnp.where(cnt >= topk, cand, best)

    kth = lax.fori_loop(0, 32, search, jnp.full((1, tq), jnp.iinfo(jnp.int32).min, jnp.int32))
    above = key > kth
    need = (topk - jnp.sum(above.astype(jnp.int32), axis=0, keepdims=True)).astype(F32)
    tie = key == kth
    tie_bf = tie.astype(BF16)
    carry = jnp.zeros((1, tq), F32)
    pre = []
    for c in range(s_len // DSA_PREFIX_CHUNK):
        blk = tie_bf[c * DSA_PREFIX_CHUNK:(c + 1) * DSA_PREFIX_CHUNK, :]
        inc = jnp.dot(tri_ref[...], blk, preferred_element_type=F32) + carry
        pre.append(inc)
        carry = inc[DSA_PREFIX_CHUNK - 1:DSA_PREFIX_CHUNK, :]
    prefix = jnp.concatenate(pre, axis=0)
    sel = vis & (above | (tie & (prefix <= need)))

    outs = []
    group = ATT_HEADS // KV_HEADS
    for g in range(KV_HEADS):
        kg = k_ref[0][:, g * HEAD_DIM:(g + 1) * HEAD_DIM].astype(BF16)
        vtg = vt_ref[0][g * HEAD_DIM:(g + 1) * HEAD_DIM, :].astype(BF16)
        for hh in range(group):
            h = g * group + hh
            qh = q_ref[0][:, h * HEAD_DIM:(h + 1) * HEAD_DIM].astype(BF16)
            s = lax.dot_general(kg, qh, _NT, preferred_element_type=F32) * (HEAD_DIM ** -0.5)
            s = jnp.where(sel, s, -jnp.inf)
            m = jnp.max(s, axis=0, keepdims=True)
            p = jnp.exp(s - m)
            denom = jnp.sum(p, axis=0, keepdims=True)
            outs.append(jnp.dot(vtg, p.astype(BF16), preferred_element_type=F32) / denom)
    o_ref[0] = jnp.concatenate(outs, axis=0).T


def dsa_prompt_pallas(q2, k2, v2, qi2, ki, wi, topk):
    B, T = q2.shape[:2]
    tq = min(DSA_Q_TILE, T)
    chunk = min(DSA_PREFIX_CHUNK, T)
    groups = DSA_GROUPS if T % (DSA_GROUPS * chunk) == 0 else 1
    tg = T // groups
    assert tg % tq == 0 and tg % chunk == 0
    vt = jnp.swapaxes(v2, 1, 2)
    wit = jnp.swapaxes(wi, 1, 2)
    tri = jnp.tril(jnp.ones((chunk, chunk), BF16))
    outs = []
    for g in range(groups):
        s_len = (g + 1) * tg
        off = g * (tg // tq)
        kern = functools.partial(_dsa_prompt_kernel, topk=topk, q_start=g * tg)
        outs.append(pl.pallas_call(
            kern,
            grid=(B, tg // tq),
            in_specs=[pl.BlockSpec((1, tq, IDX_Q_W), lambda b, i, off=off: (b, off + i, 0)),
                      pl.BlockSpec((1, s_len, IDX_DIM), lambda b, i: (b, 0, 0)),
                      pl.BlockSpec((1, IDX_HEADS, tq), lambda b, i, off=off: (b, 0, off + i)),
                      pl.BlockSpec((1, tq, ATT_Q_W), lambda b, i, off=off: (b, off + i, 0)),
                      pl.BlockSpec((1, s_len, ATT_KV_W), lambda b, i: (b, 0, 0)),
                      pl.BlockSpec((1, ATT_KV_W, s_len), lambda b, i: (b, 0, 0)),
                      pl.BlockSpec((chunk, chunk), lambda b, i: (0, 0))],
            out_specs=pl.BlockSpec((1, tq, ATT_Q_W), lambda b, i: (b, i, 0)),
            out_shape=jax.ShapeDtypeStruct((B, tg, ATT_Q_W), F32),
            compiler_params=pltpu.CompilerParams(dimension_semantics=("arbitrary", "arbitrary"),
                                                 vmem_limit_bytes=VMEM_LIMIT),
            name="dsa_prompt_g%d" % g,
        )(qi2, ki, wit, q2, k2, vt, tri))
    return outs[0] if groups == 1 else jnp.concatenate(outs, axis=1)


LANES = 128
SMALL_W = LANES
SM_A = IDX_DIM
SM_B = IDX_DIM + DN_HEADS
SM_W = IDX_DIM + 2 * DN_HEADS
IN_GROUPS = (CONV_DIM, DN_V_W, ATT_Q_W, ATT_KV_W, ATT_KV_W, IDX_Q_W, SMALL_W, D_MODEL, D_MODEL)
IN_OFFS = tuple(int(o) for o in np.cumsum((0,) + IN_GROUPS))
N_IN_PAD = IN_OFFS[-1]
ROT = HEAD_DIM // ROT_FRACTION
ROT_HALF = ROT // 2
INPROJ_ROWS = 256


def regroup_w_in(w):
    (conv, z, a, b, q, k, v, qi, ki, wi, g_dn, g_att) = split_columns(w)
    pad = jnp.zeros((w.shape[0], SMALL_W - SM_W - IDX_HEADS), w.dtype)
    small = jnp.concatenate([ki, a, b, wi, pad], axis=1)
    return jnp.concatenate([conv, z, q, k, v, qi, small, g_dn, g_att], axis=1).astype(BF16)


def rope_tables(pos, small):
    inv_freq = ROPE_THETA ** (-jnp.arange(ROT_HALF, dtype=F32) / ROT_HALF)
    ang = pos.astype(F32)[:, None] * inv_freq[None, :]
    lane = jnp.arange(LANES)
    in_head = lane if small else lane % HEAD_DIM
    idx = lane % ROT_HALF
    cos = jnp.where(in_head[None, :] < ROT, jnp.cos(ang)[:, idx], 1.0)
    sin = jnp.sin(ang)[:, idx]
    sin_a = jnp.where(in_head[None, :] < ROT_HALF, -sin, 0.0)
    sin_b = jnp.where((in_head[None, :] >= ROT_HALF) & (in_head[None, :] < ROT), sin, 0.0)
    return jnp.stack([cos, sin_a, sin_b]).astype(F32)


def _rope_lanes(x, tab_ref):
    return (x * tab_ref[0] + pltpu.roll(x, LANES - ROT_HALF, 1) * tab_ref[1]
            + pltpu.roll(x, ROT_HALF, 1) * tab_ref[2])


def _inproj_kernel(x_ref, g_ref, w_ref, tab_ref, tabs_ref,
                   conv_ref, z_ref, q_ref, k_ref, v_ref, qi_ref, sm_ref, ga_ref, gb_ref):
    x = x_ref[...]
    ms = jnp.mean(x * x, axis=-1, keepdims=True)
    hn = (x * lax.rsqrt(ms + NORM_EPS) * g_ref[...]).astype(BF16)
    outs = (conv_ref, z_ref, q_ref, k_ref, v_ref, qi_ref, sm_ref, ga_ref, gb_ref)
    roped = (False, False, True, True, False, True, False, False, False)
    for gi, (o_ref, rp) in enumerate(zip(outs, roped)):
        y = jnp.dot(hn, w_ref[:, IN_OFFS[gi]:IN_OFFS[gi + 1]], preferred_element_type=F32)
        if o_ref is sm_ref:
            o_ref[...] = _rope_lanes(y, tabs_ref)
        elif rp:
            for c in range(IN_GROUPS[gi] // LANES):
                o_ref[:, c * LANES:(c + 1) * LANES] = _rope_lanes(y[:, c * LANES:(c + 1) * LANES], tab_ref)
        else:
            o_ref[...] = y


def inproj_pallas(x2d, g, w_grouped, tab, tab_small):
    n, d = x2d.shape
    tm = min(INPROJ_ROWS, n)
    p_len = tab.shape[1]
    assert n % tm == 0 and p_len % tm == 0
    tab_spec = pl.BlockSpec((3, tm, LANES), lambda i: (0, i % (p_len // tm), 0))
    return pl.pallas_call(
        _inproj_kernel,
        grid=(n // tm,),
        in_specs=[pl.BlockSpec((tm, d), lambda i: (i, 0)),
                  pl.BlockSpec((1, d), lambda i: (0, 0)),
                  pl.BlockSpec((d, N_IN_PAD), lambda i: (0, 0)),
                  tab_spec, tab_spec],
        out_specs=[pl.BlockSpec((tm, wd), lambda i: (i, 0)) for wd in IN_GROUPS],
        out_shape=[jax.ShapeDtypeStruct((n, wd), F32) for wd in IN_GROUPS],
        compiler_params=pltpu.CompilerParams(dimension_semantics=("arbitrary",), vmem_limit_bytes=VMEM_LIMIT),
        name="inproj",
    )(x2d, g.reshape(1, d), w_grouped, tab, tab_small)


DN_SUPER = 256
DN_SOLVE_PASSES = 1


def _split_bf16(x, pieces):
    parts = []
    r = x
    for _ in range(pieces):
        p = r.astype(BF16)
        parts.append(p)
        r = r - p.astype(F32)
    return parts


def _mm(a, b, passes=1):
    if passes == 1:
        return jnp.dot(a.astype(BF16), b.astype(BF16), preferred_element_type=F32)
    pa, pb = _split_bf16(a, 2), _split_bf16(b, 2)
    out = jnp.dot(pa[0], pb[0], preferred_element_type=F32)
    out = out + jnp.dot(pa[0], pb[1], preferred_element_type=F32)
    return out + jnp.dot(pa[1], pb[0], preferred_element_type=F32)


def _ones_mm(m01, x, left):
    out = None
    for p in _split_bf16(x, 3):
        t = jnp.dot(m01, p, preferred_element_type=F32) if left else jnp.dot(p, m01, preferred_element_type=F32)
        out = t if out is None else out + t
    return out


def _softplus(x):
    return jnp.maximum(x, 0.0) + jnp.log1p(jnp.exp(-jnp.abs(x)))


def _dn_prompt_kernel(cin_ref, z_ref, sm_ref, abt_ref, cw_ref, gp_ref, gpt_ref, gain_ref,
                      y_ref, sfin_ref, s_scr, ext_scr):
    i = pl.program_id(1)
    sc = cin_ref.shape[1]
    nchunk = sc // DN_CHUNK

    @pl.when(i == 0)
    def _():
        s_scr[...] = jnp.zeros_like(s_scr)
        ext_scr[0:8, :] = jnp.zeros((8, CONV_DIM), F32)

    ext_scr[8:8 + sc, :] = cin_ref[0]

    def conv_group(col):
        acc = jnp.zeros((sc, LANES), F32)
        for w in range(CONV_W):
            lo = 8 - (CONV_W - 1) + w
            acc = acc + ext_scr[lo:lo + sc, col:col + LANES] * cw_ref[w:w + 1, col:col + LANES]
        return acc * jax.nn.sigmoid(acc)

    ri = lax.broadcasted_iota(jnp.int32, (sc, sc), 0)
    ci = lax.broadcasted_iota(jnp.int32, (sc, sc), 1)
    shift = DN_CHUNK.bit_length() - 1
    same = (ri >> shift) == (ci >> shift)
    le = same & (ci <= ri)
    lt = same & (ci < ri)
    le_bf = le.astype(BF16)
    ge_bf = (same & (ci >= ri)).astype(BF16)
    same_bf = same.astype(BF16)
    eye = (ri == ci).astype(F32)

    sm = sm_ref[0]
    g_all = -jnp.exp(gp_ref[0:1, :]) * _softplus(sm + gp_ref[1:2, :])
    beta_all = jax.nn.sigmoid(sm)
    gc_all = _ones_mm(le_bf, g_all, True)
    gs_all = _ones_mm(same_bf, g_all, True)
    g_t = -jnp.exp(gpt_ref[:, 0:1]) * _softplus(abt_ref[0] + gpt_ref[:, 1:2])
    gc_t = _ones_mm(ge_bf, g_t, False)

    heads = []
    for h in range(DN_HEADS):
        q = conv_group(h * DN_DK)
        k = conv_group(DN_QK_W + h * DN_DK)
        v = conv_group(2 * DN_QK_W + h * DN_DV)
        q = q * lax.rsqrt(jnp.sum(q * q, axis=-1, keepdims=True) + NORM_EPS) * (DN_DK ** -0.5)
        k = k * lax.rsqrt(jnp.sum(k * k, axis=-1, keepdims=True) + NORM_EPS)
        gcol = gc_all[:, SM_A + h:SM_A + h + 1]
        grow = gc_t[h:h + 1, :]
        bcol = beta_all[:, SM_B + h:SM_B + h + 1]
        decay = jnp.where(le, jnp.exp(jnp.where(le, gcol - grow, 0.0)), 0.0)
        kb = k.astype(BF16)
        kk = lax.dot_general(kb, kb, _NT, preferred_element_type=F32)
        qk = lax.dot_general(q.astype(BF16), kb, _NT, preferred_element_type=F32)
        nmat = jnp.where(lt, bcol * kk * decay, 0.0)
        aqk = qk * decay
        xinv = eye - nmat
        pw = _mm(nmat, nmat, DN_SOLVE_PASSES)
        xinv = xinv + _mm(xinv, pw, DN_SOLVE_PASSES)
        for _ in range(4):
            pw = _mm(pw, pw, DN_SOLVE_PASSES)
            xinv = xinv + _mm(xinv, pw, DN_SOLVE_PASSES)
        egc = jnp.exp(gcol)
        rhs = jnp.concatenate([v * bcol, k * (bcol * egc)], axis=1)
        sol = _mm(xinv, rhs, DN_SOLVE_PASSES)
        gtot = gs_all[:, SM_A + h:SM_A + h + 1]
        heads.append(dict(val=sol[:, :DN_DV], kcd=sol[:, DN_DV:], aqk=aqk, qdec=q * egc,
                          kdec_t=(k * jnp.exp(gtot - gcol)).T, gtot=gtot))

    outs = [[] for _ in range(DN_HEADS)]
    for c in range(nchunk):
        r0 = c * DN_CHUNK
        for h, hd in enumerate(heads):
            s_old = s_scr[h]
            v_new = hd["val"][r0:r0 + DN_CHUNK] - _mm(hd["kcd"][r0:r0 + DN_CHUNK], s_old)
            o = _mm(hd["qdec"][r0:r0 + DN_CHUNK], s_old) + _mm(hd["aqk"][r0:r0 + DN_CHUNK, r0:r0 + DN_CHUNK], v_new)
            s_scr[h] = s_old * jnp.exp(hd["gtot"][r0:r0 + 1]) + _mm(hd["kdec_t"][:, r0:r0 + DN_CHUNK], v_new)
            outs[h].append(o)

    for h in range(DN_HEADS):
        o = jnp.concatenate(outs[h], axis=0)
        on = o * lax.rsqrt(jnp.mean(o * o, axis=-1, keepdims=True) + NORM_EPS) * gain_ref[...]
        zz = z_ref[0, :, h * DN_DV:(h + 1) * DN_DV]
        y_ref[0, :, h * DN_DV:(h + 1) * DN_DV] = on * (zz * jax.nn.sigmoid(zz))

    ext_scr[0:8, :] = ext_scr[sc:sc + 8, :]

    @pl.when(i == pl.num_programs(1) - 1)
    def _():
        sfin_ref[0] = s_scr[...]


def deltanet_prompt_pallas(conv_in, z, small, conv_w, a_log, dt_bias, gain):
    B, T, _ = conv_in.shape
    sc = min(DN_SUPER, T)
    assert T % sc == 0 and sc % DN_CHUNK == 0
    abt = jnp.swapaxes(small[:, :, SM_A:SM_A + 2 * DN_HEADS], 1, 2)
    lane_par = jnp.zeros((2, LANES), F32).at[0, SM_A:SM_A + DN_HEADS].set(a_log).at[1, SM_A:SM_A + DN_HEADS].set(dt_bias)
    row_par = jnp.zeros((2 * DN_HEADS, 2), F32).at[:DN_HEADS, 0].set(a_log).at[:DN_HEADS, 1].set(dt_bias)
    return pl.pallas_call(
        _dn_prompt_kernel,
        grid=(B, T // sc),
        in_specs=[pl.BlockSpec((1, sc, CONV_DIM), lambda b, i: (b, i, 0)),
                  pl.BlockSpec((1, sc, DN_V_W), lambda b, i: (b, i, 0)),
                  pl.BlockSpec((1, sc, LANES), lambda b, i: (b, i, 0)),
                  pl.BlockSpec((1, 2 * DN_HEADS, sc), lambda b, i: (b, 0, i)),
                  pl.BlockSpec((CONV_W, CONV_DIM), lambda b, i: (0, 0)),
                  pl.BlockSpec((2, LANES), lambda b, i: (0, 0)),
                  pl.BlockSpec((2 * DN_HEADS, 2), lambda b, i: (0, 0)),
                  pl.BlockSpec((1, DN_DV), lambda b, i: (0, 0))],
        out_specs=[pl.BlockSpec((1, sc, DN_V_W), lambda b, i: (b, i, 0)),
                   pl.BlockSpec((1, DN_HEADS, DN_DK, DN_DV), lambda b, i: (b, 0, 0, 0))],
        out_shape=[jax.ShapeDtypeStruct((B, T, DN_V_W), F32),
                   jax.ShapeDtypeStruct((B, DN_HEADS, DN_DK, DN_DV), F32)],
        scratch_shapes=[pltpu.VMEM((DN_HEADS, DN_DK, DN_DV), F32), pltpu.VMEM((sc + 8, CONV_DIM), F32)],
        compiler_params=pltpu.CompilerParams(dimension_semantics=("arbitrary", "arbitrary"),
                                             vmem_limit_bytes=VMEM_LIMIT),
        name="deltanet_prompt",
    )(conv_in, z, small, abt, conv_w, lane_par, row_par, gain.reshape(1, DN_DV))


def _order_key(x):
    bits = pltpu.bitcast(x, jnp.int32)
    return bits ^ ((bits >> 31) & jnp.int32(0x7FFFFFFF))


def _count(mask):
    c = jnp.sum(mask.astype(jnp.int32), axis=0, keepdims=True)
    return jnp.sum(c, axis=1, keepdims=True)


def _dsa_sample_kernel(pt_ref, qi_ref, w_ref, kin_ref, q_ref, kn_ref, vn_ref, ci_hbm, ck_hbm, cv_hbm, o_ref,
                       ibuf, kbuf, vbuf, sc_scr, att_scr, sem_i, sem_kv, *, layer, topk):
    b = pl.program_id(0)
    nb = pl.num_programs(0)
    npg = kbuf.shape[0]
    slot = lax.rem(b, 2)

    def idx_copy(row, p, sl):
        return pltpu.make_async_copy(ci_hbm.at[pt_ref[row, p], layer], ibuf.at[sl, p], sem_i.at[sl])

    def k_copy(row, p):
        return pltpu.make_async_copy(ck_hbm.at[pt_ref[row, p], layer], kbuf.at[p], sem_kv.at[0])

    def v_copy(row, p):
        return pltpu.make_async_copy(cv_hbm.at[pt_ref[row, p], layer], vbuf.at[p], sem_kv.at[1])

    def for_pages(fn):
        def body(p, c):
            fn(p)
            return c
        lax.fori_loop(0, npg, body, 0)

    @pl.when(b == 0)
    def _():
        for_pages(lambda p: idx_copy(0, p, 0).start())

    for_pages(lambda p: k_copy(b, p).start())
    for_pages(lambda p: v_copy(b, p).start())

    @pl.when(b + 1 < nb)
    def _():
        for_pages(lambda p: idx_copy(b + 1, p, 1 - slot).start())

    for_pages(lambda p: idx_copy(b, p, slot).wait())

    qi4 = qi_ref[0].astype(BF16)
    w4 = w_ref[0] * (IDX_HEADS ** -0.5)
    for p in range(npg):
        page = ibuf[slot, p].astype(BF16)
        sp = jnp.dot(qi4, page, preferred_element_type=F32)
        sc_scr[p:p + 1, :] = jnp.sum(jnp.maximum(sp * (IDX_DIM ** -0.5), 0.0) * w4, axis=0, keepdims=True)
    s_new = jnp.sum(qi4.astype(F32) * kin_ref[0].astype(BF16).astype(F32), axis=1, keepdims=True)
    score_new = jnp.sum(jnp.maximum(s_new * (IDX_DIM ** -0.5), 0.0) * w4[:, 0:1], axis=0, keepdims=True)

    key = _order_key(sc_scr[...] + 0.0)
    key_new = _order_key(score_new + 0.0)

    def search(i, best):
        cand = best + jnp.left_shift(jnp.int32(1), 31 - i)
        cnt = _count(key >= cand) + (key_new >= cand).astype(jnp.int32)
        return jnp.where(cnt >= topk, cand, best)

    kth = lax.fori_loop(0, 32, search, jnp.full((1, 1), jnp.iinfo(jnp.int32).min, jnp.int32))
    above = key > kth
    tie = key == kth
    need = (topk - _count(above) - (key_new > kth).astype(jnp.int32)).astype(F32)
    tie_bf = tie.astype(BF16)
    n_slots = key.shape[1]
    upper = (lax.broadcasted_iota(jnp.int32, (n_slots, n_slots), 0)
             <= lax.broadcasted_iota(jnp.int32, (n_slots, n_slots), 1)).astype(BF16)
    earlier = (lax.broadcasted_iota(jnp.int32, (npg, npg), 1)
               < lax.broadcasted_iota(jnp.int32, (npg, npg), 0)).astype(BF16)
    within = jnp.dot(tie_bf, upper, preferred_element_type=F32)
    before = jnp.sum(jnp.dot(earlier, tie_bf, preferred_element_type=F32), axis=1, keepdims=True)
    sel = above | (tie & ((within + before) <= need))
    sel_new = (key_new > kth) | ((key_new == kth) & ((_count(tie).astype(F32) + 1.0) <= need))
    neg = jnp.where(sel, 0.0, -jnp.inf)

    for_pages(lambda p: k_copy(b, p).wait())
    for_pages(lambda p: v_copy(b, p).wait())
    q8 = q_ref[0].astype(BF16)
    s_self = jnp.sum(q8.astype(F32) * kn_ref[0].astype(BF16).astype(F32), axis=1, keepdims=True)
    s_self = jnp.where(sel_new, s_self * (HEAD_DIM ** -0.5), -jnp.inf)
    run_max = jnp.full((ATT_HEADS, n_slots), -jnp.inf, F32)
    for p in range(npg):
        s = jnp.dot(q8, kbuf[p].astype(BF16), preferred_element_type=F32) * (HEAD_DIM ** -0.5)
        s = s + neg[p:p + 1, :]
        att_scr[p] = s
        run_max = jnp.maximum(run_max, s)
    m = jnp.maximum(jnp.max(run_max, axis=1, keepdims=True), s_self)
    p_self = jnp.exp(s_self - m)
    denom = jnp.zeros((ATT_HEADS, n_slots), F32)
    acc = p_self * vn_ref[0].astype(BF16).astype(F32)
    for p in range(npg):
        pr = jnp.exp(att_scr[p] - m)
        denom = denom + pr
        acc = acc + lax.dot_general(pr.astype(BF16), vbuf[p].astype(BF16), _NT, preferred_element_type=F32)
    total = jnp.sum(denom, axis=1, keepdims=True) + p_self
    head_group = lax.broadcasted_iota(jnp.int32, (ATT_HEADS, HEAD_DIM), 0) // (ATT_HEADS // KV_HEADS)
    out = acc[:, 0:HEAD_DIM]
    for g in range(1, KV_HEADS):
        out = jnp.where(head_group == g, acc[:, g * HEAD_DIM:(g + 1) * HEAD_DIM], out)
    o_ref[0] = out / total


def dsa_sample_pallas(q, k_new, v_new, qi, ki_new, wi, cache_k, cache_v, cache_kidx, page_table, layer, topk):
    Bs, S = q.shape[:2]
    assert S == 1, "one new token per sequence"
    n_pool = cache_k.shape[0]
    npg = page_table.shape[1]
    group = ATT_HEADS // KV_HEADS
    qi4 = qi.reshape(Bs, IDX_HEADS, IDX_DIM)
    w_b = jnp.broadcast_to(wi.reshape(Bs, IDX_HEADS, 1), (Bs, IDX_HEADS, PAGE_SIZE))
    q4 = q.reshape(Bs, KV_HEADS, group, HEAD_DIM)
    q_bd = jnp.zeros((Bs, KV_HEADS, group, KV_HEADS, HEAD_DIM), F32)
    for g in range(KV_HEADS):
        q_bd = q_bd.at[:, g, :, g, :].set(q4[:, g])
    q_bd = q_bd.reshape(Bs, ATT_HEADS, ATT_KV_W)
    ci = jnp.transpose(cache_kidx, (0, 2, 3, 1))
    ck = jnp.transpose(cache_k, (0, 2, 3, 4, 1)).reshape(n_pool, DEPTH, ATT_KV_W, PAGE_SIZE)
    cv = jnp.transpose(cache_v, (0, 2, 3, 4, 1)).reshape(n_pool, DEPTH, ATT_KV_W, PAGE_SIZE)
    row3 = lambda w: pl.BlockSpec((1, 1, w), lambda b, pt: (b, 0, 0))
    kern = functools.partial(_dsa_sample_kernel, layer=layer, topk=topk)
    out = pl.pallas_call(
        kern,
        grid_spec=pltpu.PrefetchScalarGridSpec(
            num_scalar_prefetch=1,
            grid=(Bs,),
            in_specs=[pl.BlockSpec((1, IDX_HEADS, IDX_DIM), lambda b, pt: (b, 0, 0)),
                      pl.BlockSpec((1, IDX_HEADS, PAGE_SIZE), lambda b, pt: (b, 0, 0)),
                      row3(IDX_DIM),
                      pl.BlockSpec((1, ATT_HEADS, ATT_KV_W), lambda b, pt: (b, 0, 0)),
                      row3(ATT_KV_W), row3(ATT_KV_W),
                      pl.BlockSpec(memory_space=pl.ANY), pl.BlockSpec(memory_space=pl.ANY),
                      pl.BlockSpec(memory_space=pl.ANY)],
            out_specs=pl.BlockSpec((1, ATT_HEADS, HEAD_DIM), lambda b, pt: (b, 0, 0)),
            scratch_shapes=[pltpu.VMEM((2, npg, IDX_DIM, PAGE_SIZE), F32),
                            pltpu.VMEM((npg, ATT_KV_W, PAGE_SIZE), F32),
                            pltpu.VMEM((npg, ATT_KV_W, PAGE_SIZE), F32),
                            pltpu.VMEM((npg, PAGE_SIZE), F32),
                            pltpu.VMEM((npg, ATT_HEADS, PAGE_SIZE), F32),
                            pltpu.SemaphoreType.DMA((2,)),
                            pltpu.SemaphoreType.DMA((2,))]),
        out_shape=jax.ShapeDtypeStruct((Bs, ATT_HEADS, HEAD_DIM), F32),
        compiler_params=pltpu.CompilerParams(dimension_semantics=("arbitrary",), vmem_limit_bytes=VMEM_LIMIT),
        name="dsa_sample",
    )(page_table, qi4, w_b, ki_new, q_bd, k_new, v_new, ci, ck, cv)
    return out.reshape(Bs, S, ATT_Q_W)


def rms_norm(x, g):
    xf = x.astype(F32)
    y = xf * lax.rsqrt(jnp.mean(xf * xf, axis=-1, keepdims=True) + NORM_EPS)
    return (y * g.astype(F32)).astype(x.dtype)


def l2_normalize(x):
    return x * lax.rsqrt(jnp.sum(x * x, axis=-1, keepdims=True) + NORM_EPS)


def rope_partial(x, pos):
    rot = x.shape[-1] // ROT_FRACTION
    half = rot // 2
    inv_freq = ROPE_THETA ** (-jnp.arange(half, dtype=F32) / half)
    ang = pos.astype(F32)[:, None] * inv_freq[None, :]
    cos = jnp.cos(ang)[:, None, :]
    sin = jnp.sin(ang)[:, None, :]
    xr = x[..., :rot].astype(F32)
    x1, x2 = xr[..., :half], xr[..., half:]
    rotated = jnp.concatenate([x1 * cos - x2 * sin, x2 * cos + x1 * sin], axis=-1)
    return jnp.concatenate([rotated.astype(x.dtype), x[..., rot:]], axis=-1)


def split_columns(z):
    offsets = np.cumsum(IN_SPLITS)[:-1].tolist()
    return jnp.split(z, offsets, axis=-1)


def causal_dwconv_silu(xp, w):
    y = lax.conv_general_dilated(xp, w[:, None, :].astype(xp.dtype), window_strides=(1,), padding='VALID',
                                 dimension_numbers=('NWC', 'WIO', 'NWC'), feature_group_count=xp.shape[-1])
    return jax.nn.silu(y)


def dn_features(c, a, b, a_log, dt_bias):
    B, T, _ = c.shape
    cf = c.astype(F32)
    q = l2_normalize(cf[..., :DN_QK_W].reshape(B, T, DN_HEADS, DN_DK)) * (DN_DK ** -0.5)
    k = l2_normalize(cf[..., DN_QK_W:2 * DN_QK_W].reshape(B, T, DN_HEADS, DN_DK))
    v = cf[..., 2 * DN_QK_W:].reshape(B, T, DN_HEADS, DN_DV)
    beta = jax.nn.sigmoid(b.astype(F32))
    g = -jnp.exp(a_log.astype(F32)) * jax.nn.softplus(a.astype(F32) + dt_bias.astype(F32))
    return q, k, v, g, beta


def dn_output(o, z, gain, dtype):
    B, T = o.shape[:2]
    on = o * lax.rsqrt(jnp.mean(o * o, axis=-1, keepdims=True) + NORM_EPS) * gain.astype(F32)
    y = on * jax.nn.silu(z.astype(F32)).reshape(B, T, DN_HEADS, DN_DV)
    return y.reshape(B, T, DN_V_W).astype(dtype)


def gated_delta_chunked(q, k, v, g, beta):
    B, T, H, DK = q.shape
    DV = v.shape[-1]
    C = DN_CHUNK
    NC = T // C

    def chunks(x):
        return jnp.moveaxis(x.reshape((B, NC, C, H) + x.shape[3:]), 3, 1)

    qc, kc, vc, bc = chunks(q), chunks(k), chunks(v), chunks(beta)
    gc = jnp.cumsum(chunks(g), axis=-1)
    tri = jnp.tril(jnp.ones((C, C), bool))
    strict = jnp.tril(jnp.ones((C, C), bool), -1)
    diff = gc[..., :, None] - gc[..., None, :]
    decay = jnp.where(tri, jnp.exp(jnp.where(tri, diff, 0.0)), 0.0)
    kk = jnp.einsum('bhnid,bhnjd->bhnij', kc, kc)
    a_mat = jnp.where(strict, bc[..., :, None] * kk * decay, 0.0) + jnp.eye(C, dtype=F32)
    rhs = jnp.concatenate([vc * bc[..., None], kc * (bc * jnp.exp(gc))[..., None]], axis=-1)
    sol = lax.linalg.triangular_solve(a_mat, rhs, left_side=True, lower=True)
    val, kcd = sol[..., :DV], sol[..., DV:]
    a_qk = jnp.einsum('bhnid,bhnjd->bhnij', qc, kc) * decay
    q_dec = qc * jnp.exp(gc)[..., None]
    g_last = gc[..., -1]
    k_dec = kc * jnp.exp(g_last[..., None] - gc)[..., None]

    def step(S, xs):
        qd, kd, vl, kcd_c, aqk, gl = xs
        v_new = vl - jnp.einsum('bhck,bhkv->bhcv', kcd_c, S)
        o = jnp.einsum('bhck,bhkv->bhcv', qd, S) + jnp.einsum('bhcj,bhjv->bhcv', aqk, v_new)
        S = S * jnp.exp(gl)[..., None, None] + jnp.einsum('bhck,bhcv->bhkv', kd, v_new)
        return S, o

    xs = tuple(jnp.moveaxis(t, 2, 0) for t in (q_dec, k_dec, val, kcd, a_qk, g_last))
    S_fin, o = lax.scan(step, jnp.zeros((B, H, DK, DV), F32), xs)
    o = jnp.transpose(o, (1, 0, 3, 2, 4)).reshape(B, T, H, DV)
    return o, S_fin


def gated_delta_recurrent(q, k, v, g, beta, S0):
    def step(S, xs):
        qt, kt, vt, gt, bt = xs
        S = S * jnp.exp(gt)[..., None, None]
        pred = jnp.einsum('bhk,bhkv->bhv', kt, S)
        S = S + jnp.einsum('bhk,bhv->bhkv', kt, bt[..., None] * (vt - pred))
        return S, jnp.einsum('bhk,bhkv->bhv', qt, S)

    xs = tuple(jnp.moveaxis(t, 1, 0) for t in (q, k, v, g, beta))
    S_fin, o = lax.scan(step, S0, xs)
    return jnp.moveaxis(o, 0, 1), S_fin


def dsa_features(q, k, v, qi, ki, pos):
    B, T, _ = q.shape
    q = rope_partial(q.reshape(B, T, ATT_HEADS, HEAD_DIM), pos)
    k = rope_partial(k.reshape(B, T, KV_HEADS, HEAD_DIM), pos)
    v = v.reshape(B, T, KV_HEADS, HEAD_DIM)
    qi = rope_partial(qi.reshape(B, T, IDX_HEADS, IDX_DIM), pos)
    ki = rope_partial(ki.reshape(B, T, 1, IDX_DIM), pos)[:, :, 0]
    return q, k, v, qi, ki


def index_scores(qi, ki, wi):
    s = jax.nn.relu(jnp.einsum('bqhd,bsd->bqhs', qi, ki).astype(F32) * (IDX_DIM ** -0.5))
    return jnp.einsum('bqh,bqhs->bqs', wi.astype(F32) * (IDX_HEADS ** -0.5), s)


def sparse_attend(q, ks, vs, valid):
    B, Q, H, HD = q.shape
    qg = q.reshape(B, Q, KV_HEADS, H // KV_HEADS, HD)
    s = jnp.einsum('bqngd,bqknd->bqngk', qg, ks).astype(F32) * (HD ** -0.5)
    s = jnp.where(valid[:, :, None, None, :], s, -jnp.inf)
    p = jax.nn.softmax(s, axis=-1).astype(vs.dtype)
    o = jnp.einsum('bqngk,bqknd->bqngd', p, vs)
    return o.reshape(B, Q, H * HD)


def dsa_prompt(q, k, v, qi, ki, wi, topk):
    B, T = q.shape[:2]
    nb = T // Q_BLOCK
    bidx = jnp.arange(B)[:, None, None]
    kpos = jnp.arange(T)

    def blocks(x):
        return jnp.swapaxes(x.reshape((B, nb, Q_BLOCK) + x.shape[2:]), 0, 1)

    def one_block(args):
        qb, qib, wib, t0 = args
        tpos = t0 + jnp.arange(Q_BLOCK)
        visible = kpos[None, :] <= tpos[:, None]
        score = jnp.where(visible[None], index_scores(qib, ki, wib), -jnp.inf)
        _, sel = lax.top_k(score, topk)
        valid = sel <= tpos[None, :, None]
        return sparse_attend(qb, k[bidx, sel], v[bidx, sel], valid)

    out = lax.map(one_block, (blocks(q), blocks(qi), blocks(wi), jnp.arange(nb) * Q_BLOCK))
    return jnp.swapaxes(out, 0, 1).reshape(B, T, ATT_Q_W)


def dsa_sample(q, k_new, v_new, qi, ki_new, wi, cache_k, cache_v, cache_kidx, page_table, layer, topk):
    B, S = q.shape[:2]
    past = page_table.shape[1] * PAGE_SIZE
    slots = jnp.arange(PAGE_SIZE)
    ki_past = cache_kidx[page_table[:, :, None], slots[None, None, :], layer].reshape(B, past, IDX_DIM)
    ki_all = jnp.concatenate([ki_past.astype(ki_new.dtype), ki_new], axis=1)
    qpos = past + jnp.arange(S)
    kpos = jnp.arange(past + S)
    visible = kpos[None, :] <= qpos[:, None]
    score = jnp.where(visible[None], index_scores(qi, ki_all, wi), -jnp.inf)
    _, sel = lax.top_k(score, topk)
    in_past = sel < past
    ps = jnp.minimum(sel, past - 1)
    phys = jnp.take_along_axis(page_table, (ps // PAGE_SIZE).reshape(B, S * topk), axis=1).reshape(B, S, topk)
    slot = ps % PAGE_SIZE
    bidx = jnp.arange(B)[:, None, None]
    ns = jnp.clip(sel - past, 0, S - 1)
    sel_k = jnp.where(in_past[..., None, None], cache_k[phys, slot, layer].astype(k_new.dtype), k_new[bidx, ns])
    sel_v = jnp.where(in_past[..., None, None], cache_v[phys, slot, layer].astype(v_new.dtype), v_new[bidx, ns])
    valid = sel <= qpos[None, :, None]
    return sparse_attend(q, sel_k, sel_v, valid)


def branch_merge(o_dn, o_att, g_dn, g_att, w_dn_up, w_att_up, w_o):
    y = jax.nn.sigmoid(g_dn) * (o_dn @ w_dn_up) + jax.nn.sigmoid(g_att) * (o_att @ w_att_up)
    return y @ w_o


def mem_kv(mem, g, w_mk, w_mv):
    B, M, _ = mem.shape
    m = rms_norm(mem, g)
    return ((m @ w_mk).reshape(B, M, MEM_HEADS, MEM_HEAD_DIM), (m @ w_mv).reshape(B, M, MEM_HEADS, MEM_HEAD_DIM))


def mem_attend(x, mk, mv, g, w_mq, w_mo):
    B, T, _ = x.shape
    q = (rms_norm(x, g) @ w_mq).reshape(B, T, MEM_HEADS, MEM_HEAD_DIM)
    s = jnp.einsum('bthd,bmhd->bhtm', q, mk.astype(q.dtype)).astype(F32) * (MEM_HEAD_DIM ** -0.5)
    p = jax.nn.softmax(s, axis=-1).astype(q.dtype)
    o = jnp.einsum('bhtm,bmhd->bthd', p, mv.astype(q.dtype)).reshape(B, T, MEM_W)
    return o @ w_mo


def peer_block(h, wq, sub_keys, u, v):
    n = h.shape[0]
    q = (h @ wq).reshape(n, PEER_HEADS, 2, PEER_QDIM // 2)
    s = jnp.einsum('nhcd,hckd->nhck', q, sub_keys).astype(F32)
    s_top, i_top = lax.top_k(s, PEER_TOPK)
    cand = (s_top[:, :, 0, :, None] + s_top[:, :, 1, None, :]).reshape(n, PEER_HEADS, PEER_TOPK * PEER_TOPK)
    cidx = (i_top[:, :, 0, :, None] * N_KEYS + i_top[:, :, 1, None, :]).reshape(n, PEER_HEADS, PEER_TOPK * PEER_TOPK)
    best, pos = lax.top_k(cand, PEER_TOPK)
    eidx = jnp.take_along_axis(cidx, pos, axis=-1).reshape(n, PEER_HEADS * PEER_TOPK)
    gate = jax.nn.softmax(best, axis=-1).reshape(n, PEER_HEADS * PEER_TOPK)
    act = jax.nn.gelu(jnp.einsum('nd,ned->ne', h, u[eidx]).astype(F32), approximate=False)
    return jnp.einsum('ne,ned->nd', (gate * act).astype(h.dtype), v[eidx])


def peer_ffn(h, wq, sub_keys, u, v):
    B, T, D = h.shape
    n = B * T
    flat = h.reshape(n, D)
    if n % PEER_BLOCK == 0 and n > PEER_BLOCK:
        out = lax.map(lambda hb: peer_block(hb, wq, sub_keys, u, v), flat.reshape(n // PEER_BLOCK, PEER_BLOCK, D))
        out = out.reshape(n, D)
    else:
        out = peer_block(flat, wq, sub_keys, u, v)
    return out.reshape(B, T, D)


def kernel(x_prompt, x_sample, cache_k, cache_v, cache_kidx, state_conv, state_delta, cache_mem_k, cache_mem_v,
           page_table, mem_prompt, norm_mix, w_in, dn_conv_w, dn_a_log, dn_dt_bias, dn_norm, w_dn_up, w_att_up,
           w_o, norm_mem_q, norm_mem_kv, w_mq, w_mk, w_mv, w_mo, norm_ffn, peer_wq, peer_keys, peer_u, peer_v,
           norm_final):
    B, T, _ = x_prompt.shape
    S = x_sample.shape[1]
    past = page_table.shape[1] * PAGE_SIZE
    pos_p = jnp.arange(T)
    pos_s = past + jnp.arange(S)
    topk_p = min(TOPK_MAX, T // 4)
    topk_s = min(TOPK_MAX, (past + S) // 4)

    hp, hs = x_prompt, x_sample
    p_k, p_v, p_kidx, p_conv, p_delta, p_mk, p_mv = [], [], [], [], [], [], []
    s_k, s_v, s_kidx, s_conv, s_delta = [], [], [], [], []
    Bs = x_sample.shape[0]
    tab_p, tab_p_small = rope_tables(pos_p, False), rope_tables(pos_p, True)
    pos_rows = jnp.tile(pos_s, Bs)
    tab_s, tab_s_small = rope_tables(pos_rows, False), rope_tables(pos_rows, True)
    for l in range(DEPTH):
        w_grouped = regroup_w_in(w_in[l])
        proj = inproj_pallas(hp.reshape(B * T, D_MODEL), norm_mix[l], w_grouped, tab_p, tab_p_small)
        conv_in, z, q, k, v, qi, small, g_dn, g_att = [t.reshape(B, T, -1) for t in proj]
        o_dn, st = deltanet_prompt_pallas(conv_in, z, small, dn_conv_w[l], dn_a_log[l], dn_dt_bias[l], dn_norm[l])
        aki = small[..., :IDX_DIM]
        wi = small[..., SM_W:SM_W + IDX_HEADS]
        ak = k.reshape(B, T, KV_HEADS, HEAD_DIM)
        av = v.reshape(B, T, KV_HEADS, HEAD_DIM)
        o_att = dsa_prompt_pallas(q, k, v, qi, aki, wi, topk_p)
        cbuf = conv_in
        hp = hp + branch_merge(o_dn, o_att, g_dn, g_att, w_dn_up[l], w_att_up[l], w_o[l])
        mk, mv = mem_kv(mem_prompt, norm_mem_kv[l], w_mk[l], w_mv[l])
        hp = hp + mem_attend(hp, mk, mv, norm_mem_q[l], w_mq[l], w_mo[l])
        peer_w = peer_prepare(peer_wq[l], peer_keys[l], peer_u[l], peer_v[l])
        hp = peer_residual_pallas(hp.reshape(B * T, D_MODEL), norm_ffn[l], peer_w).reshape(B, T, D_MODEL)
        p_k.append(ak)
        p_v.append(av)
        p_kidx.append(aki)
        p_conv.append(cbuf[:, -(CONV_W - 1):])
        p_delta.append(st)
        p_mk.append(mk)
        p_mv.append(mv)

        proj = inproj_pallas(hs.reshape(Bs * S, D_MODEL), norm_mix[l], w_grouped, tab_s, tab_s_small)
        conv_in, z, q, k, v, qi, small, g_dn, g_att = [t.reshape(Bs, S, -1) for t in proj]
        a = small[..., SM_A:SM_A + DN_HEADS]
        b = small[..., SM_B:SM_B + DN_HEADS]
        aki = small[..., :IDX_DIM]
        wi = small[..., SM_W:SM_W + IDX_HEADS]
        cbuf = jnp.concatenate([state_conv[l].astype(conv_in.dtype), conv_in], axis=1)
        dq, dk, dv, dg, dbeta = dn_features(causal_dwconv_silu(cbuf, dn_conv_w[l]), a, b, dn_a_log[l], dn_dt_bias[l])
        o, st = gated_delta_recurrent(dq, dk, dv, dg, dbeta, state_delta[l].astype(F32))
        o_dn = dn_output(o, z, dn_norm[l], hs.dtype)
        ak = k.reshape(Bs, S, KV_HEADS, HEAD_DIM)
        av = v.reshape(Bs, S, KV_HEADS, HEAD_DIM)
        o_att = dsa_sample_pallas(q, k, v, qi, aki, wi, cache_k, cache_v, cache_kidx, page_table, l, topk_s)
        hs = hs + branch_merge(o_dn, o_att, g_dn, g_att, w_dn_up[l], w_att_up[l], w_o[l])
        hs = hs + mem_attend(hs, cache_mem_k[l], cache_mem_v[l], norm_mem_q[l], w_mq[l], w_mo[l])
        hs = peer_residual_pallas(hs.reshape(Bs * S, D_MODEL), norm_ffn[l], peer_w).reshape(Bs, S, D_MODEL)
        s_k.append(ak)
        s_v.append(av)
        s_kidx.append(aki)
        s_conv.append(cbuf[:, -(CONV_W - 1):])
        s_delta.append(st)

    y_prompt = pallas_rms_norm(hp.reshape(B * T, D_MODEL), norm_final, 512).reshape(B, T, D_MODEL)
    y_sample = pallas_rms_norm(hs.reshape(-1, D_MODEL), norm_final, 32).reshape(hs.shape)
    return (y_prompt, y_sample,
            jnp.stack(p_k, axis=2), jnp.stack(p_v, axis=2), jnp.stack(p_kidx, axis=2),
            jnp.stack(p_conv, axis=0), jnp.stack(p_delta, axis=0),
            jnp.stack(p_mk, axis=0), jnp.stack(p_mv, axis=0),
            jnp.stack(s_k, axis=2), jnp.stack(s_v, axis=2), jnp.stack(s_kidx, axis=2),
            jnp.stack(s_conv, axis=0), jnp.stack(s_delta, axis=0))
```

```python
import functools

import jax
import jax.numpy as jnp
import numpy as np
from jax import lax
from jax.experimental import pallas as pl
from jax.experimental.pallas import tpu as pltpu

D_MODEL = 1024
DEPTH = 2
PAGE_SIZE = 128
DN_HEADS = 4
DN_DK = 128
DN_DV = 128
CONV_W = 4
DN_CHUNK = 64
ATT_HEADS = 8
KV_HEADS = 2
HEAD_DIM = 64
IDX_HEADS = 4
IDX_DIM = 64
TOPK_MAX = 256
ROPE_THETA = 500000.0
ROT_FRACTION = 4
Q_BLOCK = 128
MEM_LEN = 256
MEM_HEADS = 4
MEM_HEAD_DIM = 128
N_KEYS = 128
N_EXPERTS = N_KEYS * N_KEYS
PEER_HEADS = 8
PEER_QDIM = 256
PEER_TOPK = 16
PEER_BLOCK = 128
NORM_EPS = 1e-6

DN_QK_W = DN_HEADS * DN_DK
DN_V_W = DN_HEADS * DN_DV
CONV_DIM = 2 * DN_QK_W + DN_V_W
ATT_Q_W = ATT_HEADS * HEAD_DIM
ATT_KV_W = KV_HEADS * HEAD_DIM
IDX_Q_W = IDX_HEADS * IDX_DIM
MEM_W = MEM_HEADS * MEM_HEAD_DIM
IN_SPLITS = (CONV_DIM, DN_V_W, DN_HEADS, DN_HEADS, ATT_Q_W, ATT_KV_W, ATT_KV_W, IDX_Q_W, IDX_DIM, IDX_HEADS,
             D_MODEL, D_MODEL)
F32 = jnp.float32


def _rms_kernel(x_ref, g_ref, o_ref):
    x = x_ref[...]
    ms = jnp.mean(x * x, axis=-1, keepdims=True)
    o_ref[...] = x * lax.rsqrt(ms + NORM_EPS) * g_ref[...]


def pallas_rms_norm(x2d, g, tm):
    n, d = x2d.shape
    return pl.pallas_call(
        _rms_kernel,
        grid=(n // tm,),
        in_specs=[pl.BlockSpec((tm, d), lambda i: (i, 0)), pl.BlockSpec((1, d), lambda i: (0, 0))],
        out_specs=pl.BlockSpec((tm, d), lambda i: (i, 0)),
        out_shape=jax.ShapeDtypeStruct((n, d), F32),
        name="rms_norm",
    )(x2d, g.reshape(1, d))


BF16 = jnp.bfloat16
_NT = (((1,), (1,)), ((), ()))
PEER_GATE_TOKENS = 256
PEER_DENSE_TOKENS = 512
PEER_EXPERT_BLOCK = 1024
VMEM_LIMIT = 48 * 1024 * 1024
_STAIR = [(a, b) for a in range(PEER_TOPK) for b in range(PEER_TOPK) if (a + 1) * (b + 1) <= PEER_TOPK]


def _peer_gate_kernel(x_ref, g_ref, wqt_ref, keys_ref, hn_ref, e1_ref, lrow_ref, e2_ref, r2_ref,
                      s_scr, rank_scr, tv_scr):
    x = x_ref[...]
    ms = jnp.mean(x * x, axis=-1, keepdims=True)
    hn = (x * lax.rsqrt(ms + NORM_EPS) * g_ref[...]).astype(BF16)
    hn_ref[...] = hn
    qt = lax.dot_general(wqt_ref[...], hn, _NT, preferred_element_type=F32)
    half = PEER_QDIM // 2
    for p in range(2 * PEER_HEADS):
        qp = qt[p * half:(p + 1) * half, :].astype(BF16)
        s_scr[p] = jnp.dot(keys_ref[p], qp, preferred_element_type=F32)

    def extract(p, carry):
        cur = s_scr[p]
        rank = jnp.full(cur.shape, float(PEER_TOPK), F32)
        tops = []
        for r in range(PEER_TOPK):
            m = jnp.max(cur, axis=0, keepdims=True)
            hit = cur == m
            rank = jnp.where(hit, float(r), rank)
            cur = jnp.where(hit, -jnp.inf, cur)
            tops.append(m)
        rank_scr[p] = rank
        tv_scr[p] = jnp.concatenate(tops, axis=0)
        return carry

    lax.fori_loop(0, 2 * PEER_HEADS, extract, 0)

    def per_head(h, carry):
        tv1 = tv_scr[2 * h]
        tv2 = tv_scr[2 * h + 1]
        m1 = tv1[0:1]
        m2 = tv2[0:1]
        rows = [tv1[a:a + 1] + tv2[b:b + 1] for (a, b) in _STAIR]
        pad = (-len(rows)) % 8
        rows += [jnp.full_like(m1, -jnp.inf)] * pad
        cand = jnp.concatenate(rows, axis=0)
        n_gt = jnp.zeros(cand.shape, F32)
        for r in rows[:len(_STAIR)]:
            n_gt = n_gt + (r > cand).astype(F32)
        thr = jnp.min(jnp.where(n_gt <= float(PEER_TOPK - 1), cand, jnp.inf), axis=0, keepdims=True)
        ex1 = jnp.exp(tv1 - m1)
        ex2 = jnp.exp(tv2 - m2)
        l_rows = []
        z = jnp.zeros_like(m1)
        for a in range(PEER_TOPK):
            sel = (tv1[a:a + 1] + tv2) >= thr
            l_rows.append(jnp.sum(sel.astype(F32), axis=0, keepdims=True))
            z = z + ex1[a:a + 1] * jnp.sum(jnp.where(sel, ex2, 0.0), axis=0, keepdims=True)
        rank1 = rank_scr[2 * h]
        rank2 = rank_scr[2 * h + 1]
        lrow = jnp.zeros(rank1.shape, F32)
        for a in range(PEER_TOPK):
            lrow = jnp.where(rank1 == float(a), l_rows[a], lrow)
        in1 = rank1 < float(PEER_TOPK)
        in2 = rank2 < float(PEER_TOPK)
        e1_ref[h] = jnp.where(in1, jnp.exp(jnp.where(in1, s_scr[2 * h] - m1, 0.0)), 0.0) / z
        lrow_ref[h] = lrow
        e2_ref[h] = jnp.where(in2, jnp.exp(jnp.where(in2, s_scr[2 * h + 1] - m2, 0.0)), 0.0).astype(BF16)
        r2_ref[h] = rank2.astype(BF16)
        return carry

    lax.fori_loop(0, PEER_HEADS, per_head, 0)


def _peer_dense_kernel(hn_ref, u_ref, vt_ref, e1_ref, lrow_ref, e2_ref, r2_ref, hp_ref, o_ref, acc_ref, wt_ref):
    j = pl.program_id(1)

    @pl.when(j == 0)
    def _():
        acc_ref[...] = jnp.zeros_like(acc_ref)

    act = lax.dot_general(u_ref[...], hn_ref[...], _NT, preferred_element_type=F32)
    rows_per_step = PEER_EXPERT_BLOCK // N_KEYS
    for ii in range(rows_per_step):
        a = act[ii * N_KEYS:(ii + 1) * N_KEYS, :]
        ga = 0.5 * a * (1.0 + lax.erf(a * (2.0 ** -0.5)))
        gate = jnp.zeros(a.shape, BF16)
        zero = jnp.zeros(a.shape, BF16)
        for h in range(PEER_HEADS):
            lr = lrow_ref[h, ii:ii + 1, :].astype(BF16)
            e1 = e1_ref[h, ii:ii + 1, :].astype(BF16)
            gate = gate + jnp.where(r2_ref[h] < lr, e2_ref[h], zero) * e1
        wt_ref[ii * N_KEYS:(ii + 1) * N_KEYS, :] = gate * ga.astype(BF16)
    acc_ref[...] += jnp.dot(vt_ref[...], wt_ref[...], preferred_element_type=F32)

    @pl.when(j == pl.num_programs(1) - 1)
    def _():
        o_ref[...] = hp_ref[...] + acc_ref[...].T


def peer_prepare(wq, sub_keys, u, v):
    wqt = wq.T.astype(BF16)
    keys = sub_keys.reshape(2 * PEER_HEADS, N_KEYS, PEER_QDIM // 2).astype(BF16)
    return wqt, keys, u.astype(BF16), v.T.astype(BF16)


def peer_residual_pallas(hp2d, g, prepared):
    n, d = hp2d.shape
    tg, td = min(PEER_GATE_TOKENS, n), min(PEER_DENSE_TOKENS, n)
    nq = PEER_HEADS * PEER_QDIM
    wqt, keys, u_bf, vt_bf = prepared
    gate_shape = jax.ShapeDtypeStruct((PEER_HEADS, N_KEYS, n), F32)
    gate_shape16 = jax.ShapeDtypeStruct((PEER_HEADS, N_KEYS, n), BF16)
    gate_spec = pl.BlockSpec((PEER_HEADS, N_KEYS, tg), lambda i: (0, 0, i))
    hn, e1, lrow, e2, r2 = pl.pallas_call(
        _peer_gate_kernel,
        grid=(n // tg,),
        in_specs=[pl.BlockSpec((tg, d), lambda i: (i, 0)),
                  pl.BlockSpec((1, d), lambda i: (0, 0)),
                  pl.BlockSpec((nq, d), lambda i: (0, 0)),
                  pl.BlockSpec((2 * PEER_HEADS, N_KEYS, PEER_QDIM // 2), lambda i: (0, 0, 0))],
        out_specs=[pl.BlockSpec((tg, d), lambda i: (i, 0)), gate_spec, gate_spec, gate_spec, gate_spec],
        out_shape=[jax.ShapeDtypeStruct((n, d), BF16), gate_shape, gate_shape, gate_shape16, gate_shape16],
        scratch_shapes=[pltpu.VMEM((2 * PEER_HEADS, N_KEYS, tg), F32),
                        pltpu.VMEM((2 * PEER_HEADS, N_KEYS, tg), F32),
                        pltpu.VMEM((2 * PEER_HEADS, PEER_TOPK, tg), F32)],
        compiler_params=pltpu.CompilerParams(dimension_semantics=("arbitrary",), vmem_limit_bytes=VMEM_LIMIT),
        name="peer_gate",
    )(hp2d, g.reshape(1, d), wqt, keys)

    rows = PEER_EXPERT_BLOCK // N_KEYS
    full_spec = pl.BlockSpec((PEER_HEADS, N_KEYS, td), lambda i, j: (0, 0, i))
    row_spec = pl.BlockSpec((PEER_HEADS, rows, td), lambda i, j: (0, j, i))
    return pl.pallas_call(
        _peer_dense_kernel,
        grid=(n // td, N_EXPERTS // PEER_EXPERT_BLOCK),
        in_specs=[pl.BlockSpec((td, d), lambda i, j: (i, 0)),
                  pl.BlockSpec((PEER_EXPERT_BLOCK, d), lambda i, j: (j, 0)),
                  pl.BlockSpec((d, PEER_EXPERT_BLOCK), lambda i, j: (0, j)),
                  row_spec, row_spec, full_spec, full_spec,
                  pl.BlockSpec((td, d), lambda i, j: (i, 0))],
        out_specs=pl.BlockSpec((td, d), lambda i, j: (i, 0)),
        out_shape=jax.ShapeDtypeStruct((n, d), F32),
        scratch_shapes=[pltpu.VMEM((d, td), F32), pltpu.VMEM((PEER_EXPERT_BLOCK, td), BF16)],
        compiler_params=pltpu.CompilerParams(dimension_semantics=("arbitrary", "arbitrary"),
                                             vmem_limit_bytes=VMEM_LIMIT),
        name="peer_dense",
    )(hn, u_bf, vt_bf, e1, lrow, e2, r2, hp2d)


DSA_Q_TILE = 128
DSA_PREFIX_CHUNK = 512
DSA_GROUPS = 4


def _dsa_prompt_kernel(qi_ref, ki_ref, wit_ref, q_ref, k_ref, vt_ref, tri_ref, o_ref, *, topk, q_start):
    tq = q_ref.shape[1]
    s_len = k_ref.shape[1]
    t0 = q_start + pl.program_id(1) * tq
    ki = ki_ref[0].astype(BF16)
    qi = qi_ref[0].astype(BF16)
    score = jnp.zeros((s_len, tq), F32)
    for h in range(IDX_HEADS):
        sh = lax.dot_general(ki, qi[:, h * IDX_DIM:(h + 1) * IDX_DIM], _NT, preferred_element_type=F32)
        score = score + jnp.maximum(sh * (IDX_DIM ** -0.5), 0.0) * (wit_ref[0, h:h + 1, :] * (IDX_HEADS ** -0.5))
    kpos = lax.broadcasted_iota(jnp.int32, (s_len, tq), 0)
    tpos = t0 + lax.broadcasted_iota(jnp.int32, (s_len, tq), 1)
    vis = kpos <= tpos
    score = jnp.where(vis, score + 0.0, -jnp.inf)
    bits = pltpu.bitcast(score, jnp.int32)
    key = bits ^ ((bits >> 31) & jnp.int32(0x7FFFFFFF))

    def search(i, best):
        cand = best + jnp.left_shift(jnp.int32(1), 31 - i)
        cnt = jnp.sum((key >= cand).astype(jnp.int32), axis=0, keepdims=True)
        return jnp.where(cnt >= topk, cand, best)

    kth = lax.fori_loop(0, 32, search, jnp.full((1, tq), jnp.iinfo(jnp.int32).min, jnp.int32))
    above = key > kth
    need = (topk - jnp.sum(above.astype(jnp.int32), axis=0, keepdims=True)).astype(F32)
    tie = key == kth
    tie_bf = tie.astype(BF16)
    carry = jnp.zeros((1, tq), F32)
    pre = []
    for c in range(s_len // DSA_PREFIX_CHUNK):
        blk = tie_bf[c * DSA_PREFIX_CHUNK:(c + 1) * DSA_PREFIX_CHUNK, :]
        inc = jnp.dot(tri_ref[...], blk, preferred_element_type=F32) + carry
        pre.append(inc)
        carry = inc[DSA_PREFIX_CHUNK - 1:DSA_PREFIX_CHUNK, :]
    prefix = jnp.concatenate(pre, axis=0)
    sel = vis & (above | (tie & (prefix <= need)))

    outs = []
    group = ATT_HEADS // KV_HEADS
    for g in range(KV_HEADS):
        kg = k_ref[0][:, g * HEAD_DIM:(g + 1) * HEAD_DIM].astype(BF16)
        vtg = vt_ref[0][g * HEAD_DIM:(g + 1) * HEAD_DIM, :].astype(BF16)
        for hh in range(group):
            h = g * group + hh
            qh = q_ref[0][:, h * HEAD_DIM:(h + 1) * HEAD_DIM].astype(BF16)
            s = lax.dot_general(kg, qh, _NT, preferred_element_type=F32) * (HEAD_DIM ** -0.5)
            s = jnp.where(sel, s, -jnp.inf)
            m = jnp.max(s, axis=0, keepdims=True)
            p = jnp.exp(s - m)
            denom = jnp.sum(p, axis=0, keepdims=True)
            outs.append(jnp.dot(vtg, p.astype(BF16), preferred_element_type=F32) / denom)
    o_ref[0] = jnp.concatenate(outs, axis=0).T


def dsa_prompt_pallas(q2, k2, v2, qi2, ki, wi, topk):
    B, T = q2.shape[:2]
    tq = min(DSA_Q_TILE, T)
    chunk = min(DSA_PREFIX_CHUNK, T)
    groups = DSA_GROUPS if T % (DSA_GROUPS * chunk) == 0 else 1
    tg = T // groups
    assert tg % tq == 0 and tg % chunk == 0
    vt = jnp.swapaxes(v2, 1, 2)
    wit = jnp.swapaxes(wi, 1, 2)
    tri = jnp.tril(jnp.ones((chunk, chunk), BF16))
    outs = []
    for g in range(groups):
        s_len = (g + 1) * tg
        off = g * (tg // tq)
        kern = functools.partial(_dsa_prompt_kernel, topk=topk, q_start=g * tg)
        outs.append(pl.pallas_call(
            kern,
            grid=(B, tg // tq),
            in_specs=[pl.BlockSpec((1, tq, IDX_Q_W), lambda b, i, off=off: (b, off + i, 0)),
                      pl.BlockSpec((1, s_len, IDX_DIM), lambda b, i: (b, 0, 0)),
                      pl.BlockSpec((1, IDX_HEADS, tq), lambda b, i, off=off: (b, 0, off + i)),
                      pl.BlockSpec((1, tq, ATT_Q_W), lambda b, i, off=off: (b, off + i, 0)),
                      pl.BlockSpec((1, s_len, ATT_KV_W), lambda b, i: (b, 0, 0)),
                      pl.BlockSpec((1, ATT_KV_W, s_len), lambda b, i: (b, 0, 0)),
                      pl.BlockSpec((chunk, chunk), lambda b, i: (0, 0))],
            out_specs=pl.BlockSpec((1, tq, ATT_Q_W), lambda b, i: (b, i, 0)),
            out_shape=jax.ShapeDtypeStruct((B, tg, ATT_Q_W), F32),
            compiler_params=pltpu.CompilerParams(dimension_semantics=("arbitrary", "arbitrary"),
                                                 vmem_limit_bytes=VMEM_LIMIT),
            name="dsa_prompt_g%d" % g,
        )(qi2, ki, wit, q2, k2, vt, tri))
    return outs[0] if groups == 1 else jnp.concatenate(outs, axis=1)


LANES = 128
SMALL_W = LANES
SM_A = IDX_DIM
SM_B = IDX_DIM + DN_HEADS
SM_W = IDX_DIM + 2 * DN_HEADS
IN_GROUPS = (CONV_DIM, DN_V_W, ATT_Q_W, ATT_KV_W, ATT_KV_W, IDX_Q_W, SMALL_W, D_MODEL, D_MODEL)
IN_OFFS = tuple(int(o) for o in np.cumsum((0,) + IN_GROUPS))
N_IN_PAD = IN_OFFS[-1]
ROT = HEAD_DIM // ROT_FRACTION
ROT_HALF = ROT // 2
INPROJ_ROWS = 256


def regroup_w_in(w):
    offsets = np.cumsum(IN_SPLITS)[:-1].tolist()
    (conv, z, a, b, q, k, v, qi, ki, wi, g_dn, g_att) = jnp.split(w.T, offsets, axis=0)
    pad = jnp.zeros((SMALL_W - SM_W - IDX_HEADS, w.shape[0]), w.dtype)
    small = jnp.concatenate([ki, a, b, wi, pad], axis=0)
    return jnp.concatenate([conv, z, q, k, v, qi, small, g_dn, g_att], axis=0).astype(BF16)


def rope_tables(pos, small):
    inv_freq = ROPE_THETA ** (-jnp.arange(ROT_HALF, dtype=F32) / ROT_HALF)
    ang = pos.astype(F32)[:, None] * inv_freq[None, :]
    lane = jnp.arange(LANES)
    in_head = lane if small else lane % HEAD_DIM
    idx = lane % ROT_HALF
    cos = jnp.where(in_head[None, :] < ROT, jnp.cos(ang)[:, idx], 1.0)
    sin = jnp.sin(ang)[:, idx]
    sin_a = jnp.where(in_head[None, :] < ROT_HALF, -sin, 0.0)
    sin_b = jnp.where((in_head[None, :] >= ROT_HALF) & (in_head[None, :] < ROT), sin, 0.0)
    return jnp.stack([cos, sin_a, sin_b]).astype(F32)


def _rope_lanes(x, tab_ref):
    return (x * tab_ref[0] + pltpu.roll(x, LANES - ROT_HALF, 1) * tab_ref[1]
            + pltpu.roll(x, ROT_HALF, 1) * tab_ref[2])


def _inproj_kernel(x_ref, g_ref, w_ref, tab_ref, tabs_ref,
                   conv_ref, z_ref, q_ref, k_ref, v_ref, qi_ref, sm_ref, ga_ref, gb_ref):
    x = x_ref[...]
    ms = jnp.mean(x * x, axis=-1, keepdims=True)
    hn = (x * lax.rsqrt(ms + NORM_EPS) * g_ref[...]).astype(BF16)
    outs = (conv_ref, z_ref, q_ref, k_ref, v_ref, qi_ref, sm_ref, ga_ref, gb_ref)
    roped = (False, False, True, True, False, True, False, False, False)
    for gi, (o_ref, rp) in enumerate(zip(outs, roped)):
        y = lax.dot_general(hn, w_ref[IN_OFFS[gi]:IN_OFFS[gi + 1], :], _NT, preferred_element_type=F32)
        if o_ref is sm_ref:
            o_ref[...] = _rope_lanes(y, tabs_ref)
        elif rp:
            for c in range(IN_GROUPS[gi] // LANES):
                o_ref[:, c * LANES:(c + 1) * LANES] = _rope_lanes(y[:, c * LANES:(c + 1) * LANES], tab_ref)
        else:
            o_ref[...] = y


def inproj_pallas(x2d, g, w_grouped, tab, tab_small):
    n, d = x2d.shape
    tm = min(INPROJ_ROWS, n)
    p_len = tab.shape[1]
    assert n % tm == 0 and p_len % tm == 0
    tab_spec = pl.BlockSpec((3, tm, LANES), lambda i: (0, i % (p_len // tm), 0))
    return pl.pallas_call(
        _inproj_kernel,
        grid=(n // tm,),
        in_specs=[pl.BlockSpec((tm, d), lambda i: (i, 0)),
                  pl.BlockSpec((1, d), lambda i: (0, 0)),
                  pl.BlockSpec((N_IN_PAD, d), lambda i: (0, 0)),
                  tab_spec, tab_spec],
        out_specs=[pl.BlockSpec((tm, wd), lambda i: (i, 0)) for wd in IN_GROUPS],
        out_shape=[jax.ShapeDtypeStruct((n, wd), F32) for wd in IN_GROUPS],
        compiler_params=pltpu.CompilerParams(dimension_semantics=("arbitrary",), vmem_limit_bytes=VMEM_LIMIT),
        name="inproj",
    )(x2d, g.reshape(1, d), w_grouped, tab, tab_small)


DN_SUPER = 256
DN_SOLVE_PASSES = 1


def _split_bf16(x, pieces):
    parts = []
    r = x
    for _ in range(pieces):
        p = r.astype(BF16)
        parts.append(p)
        r = r - p.astype(F32)
    return parts


def _mm(a, b, passes=1):
    if passes == 1:
        return jnp.dot(a.astype(BF16), b.astype(BF16), preferred_element_type=F32)
    pa, pb = _split_bf16(a, 2), _split_bf16(b, 2)
    out = jnp.dot(pa[0], pb[0], preferred_element_type=F32)
    out = out + jnp.dot(pa[0], pb[1], preferred_element_type=F32)
    return out + jnp.dot(pa[1], pb[0], preferred_element_type=F32)


def _ones_mm(m01, x, left):
    out = None
    for p in _split_bf16(x, 3):
        t = jnp.dot(m01, p, preferred_element_type=F32) if left else jnp.dot(p, m01, preferred_element_type=F32)
        out = t if out is None else out + t
    return out


def _softplus(x):
    return jnp.maximum(x, 0.0) + jnp.log1p(jnp.exp(-jnp.abs(x)))


def _dn_prompt_kernel(cin_ref, z_ref, sm_ref, abt_ref, cw_ref, gp_ref, gpt_ref, gain_ref,
                      y_ref, sfin_ref, s_scr, ext_scr):
    i = pl.program_id(1)
    sc = cin_ref.shape[1]
    nchunk = sc // DN_CHUNK

    @pl.when(i == 0)
    def _():
        s_scr[...] = jnp.zeros_like(s_scr)
        ext_scr[0:8, :] = jnp.zeros((8, CONV_DIM), F32)

    ext_scr[8:8 + sc, :] = cin_ref[0]

    def conv_group(col):
        acc = jnp.zeros((sc, LANES), F32)
        for w in range(CONV_W):
            lo = 8 - (CONV_W - 1) + w
            acc = acc + ext_scr[lo:lo + sc, col:col + LANES] * cw_ref[w:w + 1, col:col + LANES]
        return acc * jax.nn.sigmoid(acc)

    ri = lax.broadcasted_iota(jnp.int32, (sc, sc), 0)
    ci = lax.broadcasted_iota(jnp.int32, (sc, sc), 1)
    shift = DN_CHUNK.bit_length() - 1
    same = (ri >> shift) == (ci >> shift)
    le = same & (ci <= ri)
    lt = same & (ci < ri)
    le_bf = le.astype(BF16)
    ge_bf = (same & (ci >= ri)).astype(BF16)
    same_bf = same.astype(BF16)
    eye = (ri == ci).astype(F32)

    sm = sm_ref[0]
    g_all = -jnp.exp(gp_ref[0:1, :]) * _softplus(sm + gp_ref[1:2, :])
    beta_all = jax.nn.sigmoid(sm)
    gc_all = _ones_mm(le_bf, g_all, True)
    gs_all = _ones_mm(same_bf, g_all, True)
    g_t = -jnp.exp(gpt_ref[:, 0:1]) * _softplus(abt_ref[0] + gpt_ref[:, 1:2])
    gc_t = _ones_mm(ge_bf, g_t, False)

    heads = []
    for h in range(DN_HEADS):
        q = conv_group(h * DN_DK)
        k = conv_group(DN_QK_W + h * DN_DK)
        v = conv_group(2 * DN_QK_W + h * DN_DV)
        q = q * lax.rsqrt(jnp.sum(q * q, axis=-1, keepdims=True) + NORM_EPS) * (DN_DK ** -0.5)
        k = k * lax.rsqrt(jnp.sum(k * k, axis=-1, keepdims=True) + NORM_EPS)
        gcol = gc_all[:, SM_A + h:SM_A + h + 1]
        grow = gc_t[h:h + 1, :]
        bcol = beta_all[:, SM_B + h:SM_B + h + 1]
        decay = jnp.where(le, jnp.exp(jnp.where(le, gcol - grow, 0.0)), 0.0)
        kb = k.astype(BF16)
        kk = lax.dot_general(kb, kb, _NT, preferred_element_type=F32)
        qk = lax.dot_general(q.astype(BF16), kb, _NT, preferred_element_type=F32)
        nmat = jnp.where(lt, bcol * kk * decay, 0.0)
        aqk = qk * decay
        xinv = eye - nmat
        pw = _mm(nmat, nmat, DN_SOLVE_PASSES)
        xinv = xinv + _mm(xinv, pw, DN_SOLVE_PASSES)
        for _ in range(4):
            pw = _mm(pw, pw, DN_SOLVE_PASSES)
            xinv = xinv + _mm(xinv, pw, DN_SOLVE_PASSES)
        egc = jnp.exp(gcol)
        rhs = jnp.concatenate([v * bcol, k * (bcol * egc)], axis=1)
        sol = _mm(xinv, rhs, DN_SOLVE_PASSES)
        gtot = gs_all[:, SM_A + h:SM_A + h + 1]
        heads.append(dict(val=sol[:, :DN_DV], kcd=sol[:, DN_DV:], aqk=aqk, qdec=q * egc,
                          kdec_t=(k * jnp.exp(gtot - gcol)).T, gtot=gtot))

    outs = [[] for _ in range(DN_HEADS)]
    for c in range(nchunk):
        r0 = c * DN_CHUNK
        for h, hd in enumerate(heads):
            s_old = s_scr[h]
            v_new = hd["val"][r0:r0 + DN_CHUNK] - _mm(hd["kcd"][r0:r0 + DN_CHUNK], s_old)
            o = _mm(hd["qdec"][r0:r0 + DN_CHUNK], s_old) + _mm(hd["aqk"][r0:r0 + DN_CHUNK, r0:r0 + DN_CHUNK], v_new)
            s_scr[h] = s_old * jnp.exp(hd["gtot"][r0:r0 + 1]) + _mm(hd["kdec_t"][:, r0:r0 + DN_CHUNK], v_new)
            outs[h].append(o)

    for h in range(DN_HEADS):
        o = jnp.concatenate(outs[h], axis=0)
        on = o * lax.rsqrt(jnp.mean(o * o, axis=-1, keepdims=True) + NORM_EPS) * gain_ref[...]
        zz = z_ref[0, :, h * DN_DV:(h + 1) * DN_DV]
        y_ref[0, :, h * DN_DV:(h + 1) * DN_DV] = on * (zz * jax.nn.sigmoid(zz))

    ext_scr[0:8, :] = ext_scr[sc:sc + 8, :]

    @pl.when(i == pl.num_programs(1) - 1)
    def _():
        sfin_ref[0] = s_scr[...]


def deltanet_prompt_pallas(conv_in, z, small, conv_w, a_log, dt_bias, gain):
    B, T, _ = conv_in.shape
    sc = min(DN_SUPER, T)
    assert T % sc == 0 and sc % DN_CHUNK == 0
    abt = jnp.swapaxes(small[:, :, SM_A:SM_A + 2 * DN_HEADS], 1, 2)
    lane_par = jnp.zeros((2, LANES), F32).at[0, SM_A:SM_A + DN_HEADS].set(a_log).at[1, SM_A:SM_A + DN_HEADS].set(dt_bias)
    row_par = jnp.zeros((2 * DN_HEADS, 2), F32).at[:DN_HEADS, 0].set(a_log).at[:DN_HEADS, 1].set(dt_bias)
    return pl.pallas_call(
        _dn_prompt_kernel,
        grid=(B, T // sc),
        in_specs=[pl.BlockSpec((1, sc, CONV_DIM), lambda b, i: (b, i, 0)),
                  pl.BlockSpec((1, sc, DN_V_W), lambda b, i: (b, i, 0)),
                  pl.BlockSpec((1, sc, LANES), lambda b, i: (b, i, 0)),
                  pl.BlockSpec((1, 2 * DN_HEADS, sc), lambda b, i: (b, 0, i)),
                  pl.BlockSpec((CONV_W, CONV_DIM), lambda b, i: (0, 0)),
                  pl.BlockSpec((2, LANES), lambda b, i: (0, 0)),
                  pl.BlockSpec((2 * DN_HEADS, 2), lambda b, i: (0, 0)),
                  pl.BlockSpec((1, DN_DV), lambda b, i: (0, 0))],
        out_specs=[pl.BlockSpec((1, sc, DN_V_W), lambda b, i: (b, i, 0)),
                   pl.BlockSpec((1, DN_HEADS, DN_DK, DN_DV), lambda b, i: (b, 0, 0, 0))],
        out_shape=[jax.ShapeDtypeStruct((B, T, DN_V_W), F32),
                   jax.ShapeDtypeStruct((B, DN_HEADS, DN_DK, DN_DV), F32)],
        scratch_shapes=[pltpu.VMEM((DN_HEADS, DN_DK, DN_DV), F32), pltpu.VMEM((sc + 8, CONV_DIM), F32)],
        compiler_params=pltpu.CompilerParams(dimension_semantics=("arbitrary", "arbitrary"),
                                             vmem_limit_bytes=VMEM_LIMIT),
        name="deltanet_prompt",
    )(conv_in, z, small, abt, conv_w, lane_par, row_par, gain.reshape(1, DN_DV))


def _order_key(x):
    bits = pltpu.bitcast(x, jnp.int32)
    return bits ^ ((bits >> 31) & jnp.int32(0x7FFFFFFF))


def _count(mask):
    c = jnp.sum(mask.astype(jnp.int32), axis=0, keepdims=True)
    return jnp.sum(c, axis=1, keepdims=True)


def _dsa_sample_kernel(pt_ref, qi_ref, w_ref, kin_ref, q_ref, kn_ref, vn_ref, ci_hbm, ck_hbm, cv_hbm, o_ref,
                       ibuf, kbuf, vbuf, sc_scr, att_scr, sem_i, sem_kv, *, layer, topk):
    b = pl.program_id(0)
    nb = pl.num_programs(0)
    npg = kbuf.shape[0]
    slot = lax.rem(b, 2)

    def idx_copy(row, p, sl):
        return pltpu.make_async_copy(ci_hbm.at[pt_ref[row, p], layer], ibuf.at[sl, p], sem_i.at[sl])

    def k_copy(row, p):
        return pltpu.make_async_copy(ck_hbm.at[pt_ref[row, p], layer], kbuf.at[p], sem_kv.at[0])

    def v_copy(row, p):
        return pltpu.make_async_copy(cv_hbm.at[pt_ref[row, p], layer], vbuf.at[p], sem_kv.at[1])

    def for_pages(fn):
        def body(p, c):
            fn(p)
            return c
        lax.fori_loop(0, npg, body, 0)

    @pl.when(b == 0)
    def _():
        for_pages(lambda p: idx_copy(0, p, 0).start())

    for_pages(lambda p: k_copy(b, p).start())
    for_pages(lambda p: v_copy(b, p).start())

    @pl.when(b + 1 < nb)
    def _():
        for_pages(lambda p: idx_copy(b + 1, p, 1 - slot).start())

    for_pages(lambda p: idx_copy(b, p, slot).wait())

    qi4 = qi_ref[0].astype(BF16)
    w4 = w_ref[0] * (IDX_HEADS ** -0.5)
    for p in range(npg):
        page = ibuf[slot, p].astype(BF16)
        sp = jnp.dot(qi4, page, preferred_element_type=F32)
        sc_scr[p:p + 1, :] = jnp.sum(jnp.maximum(sp * (IDX_DIM ** -0.5), 0.0) * w4, axis=0, keepdims=True)
    s_new = jnp.sum(qi4.astype(F32) * kin_ref[0].astype(BF16).astype(F32), axis=1, keepdims=True)
    score_new = jnp.sum(jnp.maximum(s_new * (IDX_DIM ** -0.5), 0.0) * w4[:, 0:1], axis=0, keepdims=True)

    key = _order_key(sc_scr[...] + 0.0)
    key_new = _order_key(score_new + 0.0)

    def search(i, best):
        cand = best + jnp.left_shift(jnp.int32(1), 31 - i)
        cnt = _count(key >= cand) + (key_new >= cand).astype(jnp.int32)
        return jnp.where(cnt >= topk, cand, best)

    kth = lax.fori_loop(0, 32, search, jnp.full((1, 1), jnp.iinfo(jnp.int32).min, jnp.int32))
    above = key > kth
    tie = key == kth
    need = (topk - _count(above) - (key_new > kth).astype(jnp.int32)).astype(F32)
    tie_bf = tie.astype(BF16)
    n_slots = key.shape[1]
    upper = (lax.broadcasted_iota(jnp.int32, (n_slots, n_slots), 0)
             <= lax.broadcasted_iota(jnp.int32, (n_slots, n_slots), 1)).astype(BF16)
    earlier = (lax.broadcasted_iota(jnp.int32, (npg, npg), 1)
               < lax.broadcasted_iota(jnp.int32, (npg, npg), 0)).astype(BF16)
    within = jnp.dot(tie_bf, upper, preferred_element_type=F32)
    before = jnp.sum(jnp.dot(earlier, tie_bf, preferred_element_type=F32), axis=1, keepdims=True)
    sel = above | (tie & ((within + before) <= need))
    sel_new = (key_new > kth) | ((key_new == kth) & ((_count(tie).astype(F32) + 1.0) <= need))
    neg = jnp.where(sel, 0.0, -jnp.inf)

    for_pages(lambda p: k_copy(b, p).wait())
    for_pages(lambda p: v_copy(b, p).wait())
    q8 = q_ref[0].astype(BF16)
    s_self = jnp.sum(q8.astype(F32) * kn_ref[0].astype(BF16).astype(F32), axis=1, keepdims=True)
    s_self = jnp.where(sel_new, s_self * (HEAD_DIM ** -0.5), -jnp.inf)
    run_max = jnp.full((ATT_HEADS, n_slots), -jnp.inf, F32)
    for p in range(npg):
        s = jnp.dot(q8, kbuf[p].astype(BF16), preferred_element_type=F32) * (HEAD_DIM ** -0.5)
        s = s + neg[p:p + 1, :]
        att_scr[p] = s
        run_max = jnp.maximum(run_max, s)
    m = jnp.maximum(jnp.max(run_max, axis=1, keepdims=True), s_self)
    p_self = jnp.exp(s_self - m)
    denom = jnp.zeros((ATT_HEADS, n_slots), F32)
    acc = p_self * vn_ref[0].astype(BF16).astype(F32)
    for p in range(npg):
        pr = jnp.exp(att_scr[p] - m)
        denom = denom + pr
        acc = acc + lax.dot_general(pr.astype(BF16), vbuf[p].astype(BF16), _NT, preferred_element_type=F32)
    total = jnp.sum(denom, axis=1, keepdims=True) + p_self
    head_group = lax.broadcasted_iota(jnp.int32, (ATT_HEADS, HEAD_DIM), 0) // (ATT_HEADS // KV_HEADS)
    out = acc[:, 0:HEAD_DIM]
    for g in range(1, KV_HEADS):
        out = jnp.where(head_group == g, acc[:, g * HEAD_DIM:(g + 1) * HEAD_DIM], out)
    o_ref[0] = out / total


def dsa_sample_pallas(q, k_new, v_new, qi, ki_new, wi, cache_k, cache_v, cache_kidx, page_table, layer, topk):
    Bs, S = q.shape[:2]
    assert S == 1, "one new token per sequence"
    n_pool = cache_k.shape[0]
    npg = page_table.shape[1]
    group = ATT_HEADS // KV_HEADS
    qi4 = qi.reshape(Bs, IDX_HEADS, IDX_DIM)
    w_b = jnp.broadcast_to(wi.reshape(Bs, IDX_HEADS, 1), (Bs, IDX_HEADS, PAGE_SIZE))
    q4 = q.reshape(Bs, KV_HEADS, group, HEAD_DIM)
    q_bd = jnp.zeros((Bs, KV_HEADS, group, KV_HEADS, HEAD_DIM), F32)
    for g in range(KV_HEADS):
        q_bd = q_bd.at[:, g, :, g, :].set(q4[:, g])
    q_bd = q_bd.reshape(Bs, ATT_HEADS, ATT_KV_W)
    ci = jnp.transpose(cache_kidx, (0, 2, 3, 1))
    ck = jnp.transpose(cache_k, (0, 2, 3, 4, 1)).reshape(n_pool, DEPTH, ATT_KV_W, PAGE_SIZE)
    cv = jnp.transpose(cache_v, (0, 2, 3, 4, 1)).reshape(n_pool, DEPTH, ATT_KV_W, PAGE_SIZE)
    row3 = lambda w: pl.BlockSpec((1, 1, w), lambda b, pt: (b, 0, 0))
    kern = functools.partial(_dsa_sample_kernel, layer=layer, topk=topk)
    out = pl.pallas_call(
        kern,
        grid_spec=pltpu.PrefetchScalarGridSpec(
            num_scalar_prefetch=1,
            grid=(Bs,),
            in_specs=[pl.BlockSpec((1, IDX_HEADS, IDX_DIM), lambda b, pt: (b, 0, 0)),
                      pl.BlockSpec((1, IDX_HEADS, PAGE_SIZE), lambda b, pt: (b, 0, 0)),
                      row3(IDX_DIM),
                      pl.BlockSpec((1, ATT_HEADS, ATT_KV_W), lambda b, pt: (b, 0, 0)),
                      row3(ATT_KV_W), row3(ATT_KV_W),
                      pl.BlockSpec(memory_space=pl.ANY), pl.BlockSpec(memory_space=pl.ANY),
                      pl.BlockSpec(memory_space=pl.ANY)],
            out_specs=pl.BlockSpec((1, ATT_HEADS, HEAD_DIM), lambda b, pt: (b, 0, 0)),
            scratch_shapes=[pltpu.VMEM((2, npg, IDX_DIM, PAGE_SIZE), F32),
                            pltpu.VMEM((npg, ATT_KV_W, PAGE_SIZE), F32),
                            pltpu.VMEM((npg, ATT_KV_W, PAGE_SIZE), F32),
                            pltpu.VMEM((npg, PAGE_SIZE), F32),
                            pltpu.VMEM((npg, ATT_HEADS, PAGE_SIZE), F32),
                            pltpu.SemaphoreType.DMA((2,)),
                            pltpu.SemaphoreType.DMA((2,))]),
        out_shape=jax.ShapeDtypeStruct((Bs, ATT_HEADS, HEAD_DIM), F32),
        compiler_params=pltpu.CompilerParams(dimension_semantics=("arbitrary",), vmem_limit_bytes=VMEM_LIMIT),
        name="dsa_sample",
    )(page_table, qi4, w_b, ki_new, q_bd, k_new, v_new, ci, ck, cv)
    return out.reshape(Bs, S, ATT_Q_W)


def _dn_sample_kernel(cbuf_ref, z_ref, sm_ref, st_ref, cw_ref, gp_ref, gain_ref, y_ref, so_ref):
    cw = cw_ref[...]
    conv = jnp.sum(cbuf_ref[0] * cw, axis=0, keepdims=True)
    c = conv * jax.nn.sigmoid(conv)
    sm = sm_ref[0]
    g_all = -jnp.exp(gp_ref[0:1, :]) * _softplus(sm + gp_ref[1:2, :])
    beta_all = jax.nn.sigmoid(sm)
    for h in range(DN_HEADS):
        q = c[:, h * DN_DK:(h + 1) * DN_DK]
        k = c[:, DN_QK_W + h * DN_DK:DN_QK_W + (h + 1) * DN_DK]
        v = c[:, 2 * DN_QK_W + h * DN_DV:2 * DN_QK_W + (h + 1) * DN_DV]
        q = q * lax.rsqrt(jnp.sum(q * q, axis=-1, keepdims=True) + NORM_EPS) * (DN_DK ** -0.5)
        k = k * lax.rsqrt(jnp.sum(k * k, axis=-1, keepdims=True) + NORM_EPS)
        s = st_ref[0, h] * jnp.exp(g_all[:, SM_A + h:SM_A + h + 1])
        pred = _mm(k, s)
        upd = beta_all[:, SM_B + h:SM_B + h + 1] * (v - pred)
        k_col = jnp.broadcast_to(k, (8, DN_DK)).T[:, 0:1]
        s = s + k_col * upd
        so_ref[0, h] = s
        o = _mm(q, s)
        on = o * lax.rsqrt(jnp.mean(o * o, axis=-1, keepdims=True) + NORM_EPS) * gain_ref[...]
        zz = z_ref[0, :, h * DN_DV:(h + 1) * DN_DV]
        y_ref[0, :, h * DN_DV:(h + 1) * DN_DV] = on * (zz * jax.nn.sigmoid(zz))


def deltanet_sample_pallas(cbuf, z, small, state, conv_w, a_log, dt_bias, gain):
    Bs = cbuf.shape[0]
    lane_par = jnp.zeros((2, LANES), F32).at[0, SM_A:SM_A + DN_HEADS].set(a_log).at[1, SM_A:SM_A + DN_HEADS].set(dt_bias)
    return pl.pallas_call(
        _dn_sample_kernel,
        grid=(Bs,),
        in_specs=[pl.BlockSpec((1, CONV_W, CONV_DIM), lambda b: (b, 0, 0)),
                  pl.BlockSpec((1, 1, DN_V_W), lambda b: (b, 0, 0)),
                  pl.BlockSpec((1, 1, LANES), lambda b: (b, 0, 0)),
                  pl.BlockSpec((1, DN_HEADS, DN_DK, DN_DV), lambda b: (b, 0, 0, 0)),
                  pl.BlockSpec((CONV_W, CONV_DIM), lambda b: (0, 0)),
                  pl.BlockSpec((2, LANES), lambda b: (0, 0)),
                  pl.BlockSpec((1, DN_DV), lambda b: (0, 0))],
        out_specs=[pl.BlockSpec((1, 1, DN_V_W), lambda b: (b, 0, 0)),
                   pl.BlockSpec((1, DN_HEADS, DN_DK, DN_DV), lambda b: (b, 0, 0, 0))],
        out_shape=[jax.ShapeDtypeStruct((Bs, 1, DN_V_W), F32),
                   jax.ShapeDtypeStruct((Bs, DN_HEADS, DN_DK, DN_DV), F32)],
        compiler_params=pltpu.CompilerParams(dimension_semantics=("arbitrary",), vmem_limit_bytes=VMEM_LIMIT),
        name="deltanet_sample",
    )(cbuf, z, small, state, conv_w, lane_par, gain.reshape(1, DN_DV))


MERGE_ROWS = 256


def _merge_mem_kernel(hp_ref, odn_ref, oatt_ref, gdn_ref, gatt_ref, wdn_ref, watt_ref, wo_ref, gq_ref, wmq_ref,
                      mk_ref, mv_ref, wmo_ref, o_ref):
    y = (jax.nn.sigmoid(gdn_ref[0]) * jnp.dot(odn_ref[0].astype(BF16), wdn_ref[...], preferred_element_type=F32)
         + jax.nn.sigmoid(gatt_ref[0]) * jnp.dot(oatt_ref[0].astype(BF16), watt_ref[...], preferred_element_type=F32))
    h1 = hp_ref[0] + jnp.dot(y.astype(BF16), wo_ref[...], preferred_element_type=F32)
    ms = jnp.mean(h1 * h1, axis=-1, keepdims=True)
    qn = (h1 * lax.rsqrt(ms + NORM_EPS) * gq_ref[...]).astype(BF16)
    q = jnp.dot(qn, wmq_ref[...], preferred_element_type=F32)
    outs = []
    for h in range(MEM_HEADS):
        hs = slice(h * MEM_HEAD_DIM, (h + 1) * MEM_HEAD_DIM)
        s = lax.dot_general(q[:, hs].astype(BF16), mk_ref[0, :, hs].astype(BF16), _NT,
                            preferred_element_type=F32) * (MEM_HEAD_DIM ** -0.5)
        s = s - jnp.max(s, axis=-1, keepdims=True)
        p = jnp.exp(s)
        p = p / jnp.sum(p, axis=-1, keepdims=True)
        outs.append(jnp.dot(p.astype(BF16), mv_ref[0, :, hs].astype(BF16), preferred_element_type=F32))
    o = jnp.concatenate(outs, axis=1).astype(BF16)
    o_ref[0] = h1 + jnp.dot(o, wmo_ref[...], preferred_element_type=F32)


def merge_mem_pallas(hp, o_dn, o_att, g_dn, g_att, w_dn_up, w_att_up, w_o, g_q, w_mq, mk, mv, w_mo):
    B, T, d = hp.shape
    tm = min(MERGE_ROWS, T)
    assert T % tm == 0
    row = lambda w: pl.BlockSpec((1, tm, w), lambda b, i: (b, i, 0))
    full = lambda a: pl.BlockSpec(a.shape, lambda b, i: (0,) * a.ndim)
    mem = pl.BlockSpec((1, MEM_LEN, MEM_W), lambda b, i: (b, 0, 0))
    ws = [w.astype(BF16) for w in (w_dn_up, w_att_up, w_o)]
    gq = g_q.reshape(1, d)
    wmq, wmo = w_mq.astype(BF16), w_mo.astype(BF16)
    return pl.pallas_call(
        _merge_mem_kernel,
        grid=(B, T // tm),
        in_specs=[row(d), row(DN_V_W), row(ATT_Q_W), row(d), row(d), full(ws[0]), full(ws[1]), full(ws[2]),
                  full(gq), full(wmq), mem, mem, full(wmo)],
        out_specs=row(d),
        out_shape=jax.ShapeDtypeStruct((B, T, d), F32),
        compiler_params=pltpu.CompilerParams(dimension_semantics=("arbitrary", "arbitrary"),
                                             vmem_limit_bytes=VMEM_LIMIT),
        name="merge_mem",
    )(hp, o_dn, o_att, g_dn, g_att, ws[0], ws[1], ws[2], gq, wmq, mk, mv, wmo)


def _mem_kv_kernel(m_ref, g_ref, wk_ref, wv_ref, k_ref, v_ref):
    x = m_ref[...]
    ms = jnp.mean(x * x, axis=-1, keepdims=True)
    xn = (x * lax.rsqrt(ms + NORM_EPS) * g_ref[...]).astype(BF16)
    k_ref[...] = jnp.dot(xn, wk_ref[...], preferred_element_type=F32)
    v_ref[...] = jnp.dot(xn, wv_ref[...], preferred_element_type=F32)


def mem_kv_pallas(mem2d, g, w_mk, w_mv):
    n, d = mem2d.shape
    tm = min(MERGE_ROWS, n)
    wk, wv = w_mk.astype(BF16), w_mv.astype(BF16)
    return pl.pallas_call(
        _mem_kv_kernel,
        grid=(n // tm,),
        in_specs=[pl.BlockSpec((tm, d), lambda i: (i, 0)), pl.BlockSpec((1, d), lambda i: (0, 0)),
                  pl.BlockSpec(wk.shape, lambda i: (0, 0)), pl.BlockSpec(wv.shape, lambda i: (0, 0))],
        out_specs=[pl.BlockSpec((tm, MEM_W), lambda i: (i, 0)), pl.BlockSpec((tm, MEM_W), lambda i: (i, 0))],
        out_shape=[jax.ShapeDtypeStruct((n, MEM_W), F32), jax.ShapeDtypeStruct((n, MEM_W), F32)],
        compiler_params=pltpu.CompilerParams(dimension_semantics=("arbitrary",), vmem_limit_bytes=VMEM_LIMIT),
        name="mem_kv",
    )(mem2d, g.reshape(1, d), wk, wv)


def kernel(x_prompt, x_sample, cache_k, cache_v, cache_kidx, state_conv, state_delta, cache_mem_k, cache_mem_v,
           page_table, mem_prompt, norm_mix, w_in, dn_conv_w, dn_a_log, dn_dt_bias, dn_norm, w_dn_up, w_att_up,
           w_o, norm_mem_q, norm_mem_kv, w_mq, w_mk, w_mv, w_mo, norm_ffn, peer_wq, peer_keys, peer_u, peer_v,
           norm_final):
    B, T, _ = x_prompt.shape
    S = x_sample.shape[1]
    past = page_table.shape[1] * PAGE_SIZE
    pos_p = jnp.arange(T)
    pos_s = past + jnp.arange(S)
    topk_p = min(TOPK_MAX, T // 4)
    topk_s = min(TOPK_MAX, (past + S) // 4)

    hp, hs = x_prompt, x_sample
    p_k, p_v, p_kidx, p_conv, p_delta, p_mk, p_mv = [], [], [], [], [], [], []
    s_k, s_v, s_kidx, s_conv, s_delta = [], [], [], [], []
    Bs = x_sample.shape[0]
    tab_p, tab_p_small = rope_tables(pos_p, False), rope_tables(pos_p, True)
    pos_rows = jnp.tile(pos_s, Bs)
    tab_s, tab_s_small = rope_tables(pos_rows, False), rope_tables(pos_rows, True)
    for l in range(DEPTH):
        w_grouped = regroup_w_in(w_in[l])
        proj = inproj_pallas(hp.reshape(B * T, D_MODEL), norm_mix[l], w_grouped, tab_p, tab_p_small)
        conv_in, z, q, k, v, qi, small, g_dn, g_att = [t.reshape(B, T, -1) for t in proj]
        o_dn, st = deltanet_prompt_pallas(conv_in, z, small, dn_conv_w[l], dn_a_log[l], dn_dt_bias[l], dn_norm[l])
        aki = small[..., :IDX_DIM]
        wi = small[..., SM_W:SM_W + IDX_HEADS]
        ak = k.reshape(B, T, KV_HEADS, HEAD_DIM)
        av = v.reshape(B, T, KV_HEADS, HEAD_DIM)
        o_att = dsa_prompt_pallas(q, k, v, qi, aki, wi, topk_p)
        cbuf = conv_in
        mk2, mv2 = mem_kv_pallas(mem_prompt.reshape(B * MEM_LEN, D_MODEL), norm_mem_kv[l], w_mk[l], w_mv[l])
        mk = mk2.reshape(B, MEM_LEN, MEM_HEADS, MEM_HEAD_DIM)
        mv = mv2.reshape(B, MEM_LEN, MEM_HEADS, MEM_HEAD_DIM)
        hp = merge_mem_pallas(hp, o_dn, o_att, g_dn, g_att, w_dn_up[l], w_att_up[l], w_o[l], norm_mem_q[l], w_mq[l],
                              mk2.reshape(B, MEM_LEN, MEM_W), mv2.reshape(B, MEM_LEN, MEM_W), w_mo[l])
        peer_w = peer_prepare(peer_wq[l], peer_keys[l], peer_u[l], peer_v[l])
        hp = peer_residual_pallas(hp.reshape(B * T, D_MODEL), norm_ffn[l], peer_w).reshape(B, T, D_MODEL)
        p_k.append(ak)
        p_v.append(av)
        p_kidx.append(aki)
        p_conv.append(cbuf[:, -(CONV_W - 1):])
        p_delta.append(st)
        p_mk.append(mk)
        p_mv.append(mv)

        proj = inproj_pallas(hs.reshape(Bs * S, D_MODEL), norm_mix[l], w_grouped, tab_s, tab_s_small)
        conv_in, z, q, k, v, qi, small, g_dn, g_att = [t.reshape(Bs, S, -1) for t in proj]
        aki = small[..., :IDX_DIM]
        wi = small[..., SM_W:SM_W + IDX_HEADS]
        cbuf = jnp.concatenate([state_conv[l].astype(conv_in.dtype), conv_in], axis=1)
        o_dn, st = deltanet_sample_pallas(cbuf, z, small, state_delta[l].astype(F32), dn_conv_w[l], dn_a_log[l],
                                          dn_dt_bias[l], dn_norm[l])
        ak = k.reshape(Bs, S, KV_HEADS, HEAD_DIM)
        av = v.reshape(Bs, S, KV_HEADS, HEAD_DIM)
        o_att = dsa_sample_pallas(q, k, v, qi, aki, wi, cache_k, cache_v, cache_kidx, page_table, l, topk_s)
        hs = merge_mem_pallas(hs, o_dn, o_att, g_dn, g_att, w_dn_up[l], w_att_up[l], w_o[l], norm_mem_q[l], w_mq[l],
                              cache_mem_k[l].reshape(Bs, MEM_LEN, MEM_W), cache_mem_v[l].reshape(Bs, MEM_LEN, MEM_W),
                              w_mo[l])
        hs = peer_residual_pallas(hs.reshape(Bs * S, D_MODEL), norm_ffn[l], peer_w).reshape(Bs, S, D_MODEL)
        s_k.append(ak)
        s_v.append(av)
        s_kidx.append(aki)
        s_conv.append(cbuf[:, -(CONV_W - 1):])
        s_delta.append(st)

    y_prompt = pallas_rms_norm(hp.reshape(B * T, D_MODEL), norm_final, 512).reshape(B, T, D_MODEL)
    y_sample = pallas_rms_norm(hs.reshape(-1, D_MODEL), norm_final, 32).reshape(hs.shape)
    return (y_prompt, y_sample,
            jnp.stack(p_k, axis=2), jnp.stack(p_v, axis=2), jnp.stack(p_kidx, axis=2),
            jnp.stack(p_conv, axis=0), jnp.stack(p_delta, axis=0),
            jnp.stack(p_mk, axis=0), jnp.stack(p_mv, axis=0),
            jnp.stack(s_k, axis=2), jnp.stack(s_v, axis=2), jnp.stack(s_kidx, axis=2),
            jnp.stack(s_conv, axis=0), jnp.stack(s_delta, axis=0))
```
